```python
import math
import jax, jax.numpy as jnp
from jax import lax
import numpy as np

D_MODEL = 2048
BATCH = 8
SEQ = 4096
DEPTH = 4

CTX_LEN = 256
GRID_W = 64
N_EVEN = (DEPTH + 1) // 2
N_ODD = DEPTH // 2
D_FF = 5632
N_MOD = 9
EPS = 1e-6
CONV_A_CH = D_MODEL // 2
CONV_B_CH = D_MODEL // 2
CONV_A_WIDTH = 31
CONV_B_WIDTH = 3
CONV_IN = 2 * CONV_A_CH + 3 * CONV_B_CH
CONV_OUT_IN = CONV_A_CH + CONV_B_CH
SSM_EXPAND = 2
D_INNER = SSM_EXPAND * D_MODEL
HEAD_DIM = 64
N_SSM_HEADS = D_INNER // HEAD_DIM
D_STATE = 128
N_GROUPS = 8
HEADS_PER_GROUP = N_SSM_HEADS // N_GROUPS
SSM_CONV_WIDTH = 4
CHUNK = 128
GN = N_GROUPS * D_STATE
SSM_CONV_DIM = D_INNER + 2 * GN
SSM_DIR_COLS = 2 * GN + N_SSM_HEADS
SSM_IN = 2 * D_INNER + 2 * SSM_DIR_COLS

kernel_name = "hybrid_conv_ssd_diffusion_trunk"


def rmsnorm(h, g):
    h32 = h.astype(jnp.float32)
    out = h32 * lax.rsqrt(jnp.mean(h32 * h32, axis=-1, keepdims=True) + EPS)
    return (out * g.astype(jnp.float32)).astype(h.dtype)


def layernorm(h, g, b):
    h32 = h.astype(jnp.float32)
    mu = jnp.mean(h32, axis=-1, keepdims=True)
    var = jnp.mean(jnp.square(h32 - mu), axis=-1, keepdims=True)
    out = (h32 - mu) * lax.rsqrt(var + EPS) * g.astype(jnp.float32) + b.astype(jnp.float32)
    return out.astype(h.dtype)


def modulate(h, g, shift, scale):
    return rmsnorm(h, g) * (1 + scale) + shift


def swiglu(h, w1, w3, w2):
    return (jax.nn.silu(h @ w1) * (h @ w3)) @ w2


def dwconv(u, w, pad):
    return lax.conv_general_dilated(
        u, w[:, None, :].astype(u.dtype), window_strides=(1,), padding=[pad],
        dimension_numbers=("NWC", "WIO", "NWC"), feature_group_count=u.shape[-1])


def to_col_major(h, rows):
    b, L, d = h.shape
    return h.reshape(b, rows, GRID_W, d).transpose(0, 2, 1, 3).reshape(b, L, d)


def from_col_major(h, rows):
    b, L, d = h.shape
    return h.reshape(b, GRID_W, rows, d).transpose(0, 2, 1, 3).reshape(b, L, d)


def conv_mixer_seq(h, in_w, dwa_w, dwa_b, ln_g, ln_b, dwb_w, out_w):
    u = h @ in_w
    a_val, a_gate, b_gate, c_gate, v = jnp.split(
        u, [CONV_A_CH, 2 * CONV_A_CH, 2 * CONV_A_CH + CONV_B_CH, 2 * CONV_A_CH + 2 * CONV_B_CH], axis=-1)
    a = a_val * jax.nn.sigmoid(a_gate)
    a = dwconv(a, dwa_w, (CONV_A_WIDTH // 2, CONV_A_WIDTH // 2)) + dwa_b
    a = jax.nn.silu(layernorm(a, ln_g, ln_b))
    bq = b_gate * dwconv(c_gate * v, dwb_w, (CONV_B_WIDTH // 2, CONV_B_WIDTH // 2))
    return jnp.concatenate([a, bq], axis=-1) @ out_w


def ssd_chunked(xs, dt, A, Bm, Cm, h0):
    bt, L = xs.shape[:2]
    nc = L // CHUNK
    xdt = xs.astype(jnp.float32) * dt[..., None]
    a = dt * A
    xdt = jnp.moveaxis(xdt.reshape(bt, nc, CHUNK, N_GROUPS, HEADS_PER_GROUP, HEAD_DIM), 1, 0)
    a = jnp.moveaxis(a.reshape(bt, nc, CHUNK, N_GROUPS, HEADS_PER_GROUP), 1, 0)
    Bc = jnp.moveaxis(Bm.astype(jnp.float32).reshape(bt, nc, CHUNK, N_GROUPS, D_STATE), 1, 0)
    Cc = jnp.moveaxis(Cm.astype(jnp.float32).reshape(bt, nc, CHUNK, N_GROUPS, D_STATE), 1, 0)
    causal = jnp.tril(jnp.ones((CHUNK, CHUNK), dtype=bool))

    def body(h, inp):
        xc, ac, bc, cc = inp
        cum = jnp.moveaxis(jnp.cumsum(ac, axis=1), 1, -1)
        seg = cum[..., :, None] - cum[..., None, :]
        Lm = jnp.exp(jnp.where(causal, seg, -jnp.inf))
        scores = jnp.einsum("btgn,bsgn->bgts", cc, bc)
        y = jnp.einsum("bgts,bghts,bsghp->btghp", scores, Lm, xc)
        y = y + jnp.einsum("btgn,bghpn,bght->btghp", cc, h, jnp.exp(cum))
        last = cum[..., -1:]
        h_new = h * jnp.exp(last)[..., None] + jnp.einsum(
            "bsgn,bghs,bsghp->bghpn", bc, jnp.exp(last - cum), xc)
        return h_new, y

    h_fin, ys = lax.scan(body, h0, (xdt, a, Bc, Cc))
    y = jnp.moveaxis(ys, 0, 1).reshape(bt, L, N_SSM_HEADS, HEAD_DIM)
    return y, h_fin


def ssd_direction(xbc_c, dt_c, xbc_l, dt_l, conv_w, conv_b, dt_bias, a_log, d_skip):
    A = -jnp.exp(a_log.astype(jnp.float32))

    def prep(xbc, dt_raw):
        bt, L, _ = xbc.shape
        xbc = jax.nn.silu(dwconv(xbc, conv_w, (SSM_CONV_WIDTH - 1, 0)) + conv_b)
        xs, Bm, Cm = jnp.split(xbc, [D_INNER, D_INNER + GN], axis=-1)
        xs = xs.reshape(bt, L, N_SSM_HEADS, HEAD_DIM)
        Bm = Bm.reshape(bt, L, N_GROUPS, D_STATE)
        Cm = Cm.reshape(bt, L, N_GROUPS, D_STATE)
        dt = jax.nn.softplus(dt_raw.astype(jnp.float32) + dt_bias.astype(jnp.float32))
        return xs, dt, Bm, Cm

    xs_c, dtc, B_c, C_c = prep(xbc_c, dt_c)
    xs_l, dtl, B_l, C_l = prep(xbc_l, dt_l)
    h0 = jnp.zeros((xs_c.shape[0], N_GROUPS, HEADS_PER_GROUP, HEAD_DIM, D_STATE), jnp.float32)
    y_c, h_c = ssd_chunked(xs_c, dtc, A, B_c, C_c, h0)
    y_l, _ = ssd_chunked(xs_l, dtl, A, B_l, C_l, h_c)
    skip = d_skip.astype(jnp.float32)[:, None]
    return (y_c + skip * xs_c.astype(jnp.float32),
            y_l + skip * xs_l.astype(jnp.float32))


def ssm_mixer(h_l, h_c, in_w, conv_w, conv_b, dt_bias, a_log, d_skip, norm_g, out_w):
    u_l = h_l @ in_w
    u_c = h_c @ in_w
    split_pts = [D_INNER, 2 * D_INNER, 2 * D_INNER + SSM_DIR_COLS]
    z_l, xr_l, f_l, b_l = jnp.split(u_l, split_pts, axis=-1)
    z_c, xr_c, f_c, b_c = jnp.split(u_c, split_pts, axis=-1)

    def dir_inputs(xr, cols):
        bc, dt = jnp.split(cols, [2 * GN], axis=-1)
        return jnp.concatenate([xr, bc], axis=-1), dt

    flip = lambda t: jnp.flip(t, axis=1)
    xcf, dcf = dir_inputs(xr_c, f_c)
    xlf, dlf = dir_inputs(xr_l, f_l)
    xcb, dcb = dir_inputs(xr_c, b_c)
    xlb, dlb = dir_inputs(xr_l, b_l)
    yf_c, yf_l = ssd_direction(xcf, dcf, xlf, dlf,
                               conv_w[0], conv_b[0], dt_bias[0], a_log[0], d_skip[0])
    yb_c, yb_l = ssd_direction(flip(xcb), flip(dcb), flip(xlb), flip(dlb),
                               conv_w[1], conv_b[1], dt_bias[1], a_log[1], d_skip[1])
    y_c = yf_c + flip(yb_c)
    y_l = yf_l + flip(yb_l)

    def out(y, z):
        bt, L = y.shape[:2]
        y = y.reshape(bt, L, D_INNER) * jax.nn.silu(z.astype(jnp.float32))
        y = y.reshape(bt, L, N_GROUPS, D_INNER // N_GROUPS)
        y = y * lax.rsqrt(jnp.mean(y * y, axis=-1, keepdims=True) + EPS)
        y = y.reshape(bt, L, D_INNER) * norm_g.astype(jnp.float32)
        return y.astype(z.dtype) @ out_w

    return out(y_l, z_l), out(y_c, z_c)


def setup_inputs(seed: int = 0) -> dict:
    key = jax.random.key(seed)
    ks = jax.random.split(key, 32)
    f32 = jnp.float32

    def nrm(k, shape, scale):
        return jax.random.normal(k, shape, f32) * scale

    dt0 = jnp.exp(jax.random.uniform(ks[24], (N_ODD, 2, N_SSM_HEADS), f32,
                                     minval=math.log(1e-3), maxval=math.log(1e-1)))
    return {
        "x": nrm(ks[0], (BATCH, SEQ, D_MODEL), 1.0),
        "c": nrm(ks[1], (BATCH, D_MODEL), 1.0),
        "ctx": nrm(ks[2], (BATCH, CTX_LEN, D_MODEL), 1.0),
        "c_ctx": nrm(ks[3], (D_MODEL,), 1.0),
        "mod_w": nrm(ks[4], (DEPTH, D_MODEL, N_MOD * D_MODEL), 0.5 * D_MODEL ** -0.5),
        "mod_b": nrm(ks[5], (DEPTH, N_MOD * D_MODEL), 0.01),
        "norm_g": 1.0 + nrm(ks[6], (DEPTH, 3, D_MODEL), 0.02),
        "ffn_w1": nrm(ks[7], (DEPTH, 2, D_MODEL, D_FF), D_MODEL ** -0.5),
        "ffn_w3": nrm(ks[8], (DEPTH, 2, D_MODEL, D_FF), D_MODEL ** -0.5),
        "ffn_w2": nrm(ks[9], (DEPTH, 2, D_FF, D_MODEL), D_FF ** -0.5),
        "conv_in_w": nrm(ks[10], (N_EVEN, D_MODEL, CONV_IN), D_MODEL ** -0.5),
        "conv_a_w": nrm(ks[11], (N_EVEN, CONV_A_WIDTH, CONV_A_CH), CONV_A_WIDTH ** -0.5),
        "conv_a_b": nrm(ks[12], (N_EVEN, CONV_A_CH), 0.01),
        "conv_ln_g": 1.0 + nrm(ks[13], (N_EVEN, CONV_A_CH), 0.02),
        "conv_ln_b": nrm(ks[14], (N_EVEN, CONV_A_CH), 0.01),
        "conv_b_w": nrm(ks[15], (N_EVEN, CONV_B_WIDTH, CONV_B_CH), CONV_B_WIDTH ** -0.5),
        "conv_out_w": nrm(ks[16], (N_EVEN, CONV_OUT_IN, D_MODEL), CONV_OUT_IN ** -0.5),
        "ssm_in_w": nrm(ks[17], (N_ODD, D_MODEL, SSM_IN), D_MODEL ** -0.5),
        "ssm_conv_w": nrm(ks[18], (N_ODD, 2, SSM_CONV_WIDTH, SSM_CONV_DIM), SSM_CONV_WIDTH ** -0.5),
        "ssm_conv_b": nrm(ks[19], (N_ODD, 2, SSM_CONV_DIM), 0.01),
        "ssm_dt_bias": dt0 + jnp.log(-jnp.expm1(-dt0)),
        "ssm_a_log": jnp.log(jax.random.uniform(ks[20], (N_ODD, 2, N_SSM_HEADS), f32, minval=1.0, maxval=16.0)),
        "ssm_d": 1.0 + nrm(ks[21], (N_ODD, 2, N_SSM_HEADS), 0.1),
        "ssm_norm_g": 1.0 + nrm(ks[22], (N_ODD, D_INNER), 0.02),
        "ssm_out_w": nrm(ks[23], (N_ODD, D_INNER, D_MODEL), D_INNER ** -0.5),
        "final_g": 1.0 + nrm(ks[25], (D_MODEL,), 0.02),
    }


def reference(x, c, ctx, c_ctx, mod_w, mod_b, norm_g, ffn_w1, ffn_w3, ffn_w2,
              conv_in_w, conv_a_w, conv_a_b, conv_ln_g, conv_ln_b, conv_b_w, conv_out_w,
              ssm_in_w, ssm_conv_w, ssm_conv_b, ssm_dt_bias, ssm_a_log, ssm_d, ssm_norm_g,
              ssm_out_w, final_g):
    bt = x.shape[0]
    rows = x.shape[1] // GRID_W
    h_l, h_c = x, ctx
    silu_c = jax.nn.silu(c)
    silu_cc = jax.nn.silu(c_ctx)
    for i in range(DEPTH):
        last = i == DEPTH - 1
        mod_l = (silu_c @ mod_w[i] + mod_b[i]).reshape(bt, N_MOD, 1, D_MODEL)
        mod_c = (silu_cc @ mod_w[i] + mod_b[i]).reshape(N_MOD, 1, 1, D_MODEL)
        ml = [mod_l[:, k] for k in range(N_MOD)]
        mc = [mod_c[k] for k in range(N_MOD)]

        w1a, w3a, w2a = ffn_w1[i, 0], ffn_w3[i, 0], ffn_w2[i, 0]
        h_l = h_l + 0.5 * ml[2] * swiglu(modulate(h_l, norm_g[i, 0], ml[0], ml[1]), w1a, w3a, w2a)
        h_c = h_c + 0.5 * mc[2] * swiglu(modulate(h_c, norm_g[i, 0], mc[0], mc[1]), w1a, w3a, w2a)

        n_l = modulate(h_l, norm_g[i, 1], ml[3], ml[4])
        n_c = modulate(h_c, norm_g[i, 1], mc[3], mc[4])
        if i % 2 == 0:
            e = i // 2
            cp = (conv_in_w[e], conv_a_w[e], conv_a_b[e], conv_ln_g[e], conv_ln_b[e],
                  conv_b_w[e], conv_out_w[e])
            h_l = h_l + ml[5] * conv_mixer_seq(n_l, *cp)
            if not last:
                h_c = h_c + mc[5] * conv_mixer_seq(n_c, *cp)
        else:
            o = i // 2
            col_major = (o % 2 == 1)
            if col_major:
                n_l = to_col_major(n_l, rows)
            y_l, y_c = ssm_mixer(n_l, n_c, ssm_in_w[o], ssm_conv_w[o], ssm_conv_b[o],
                                 ssm_dt_bias[o], ssm_a_log[o], ssm_d[o], ssm_norm_g[o], ssm_out_w[o])
            if col_major:
                y_l = from_col_major(y_l, rows)
            h_l = h_l + ml[5] * y_l
            if not last:
                h_c = h_c + mc[5] * y_c

        w1b, w3b, w2b = ffn_w1[i, 1], ffn_w3[i, 1], ffn_w2[i, 1]
        h_l = h_l + 0.5 * ml[8] * swiglu(modulate(h_l, norm_g[i, 2], ml[6], ml[7]), w1b, w3b, w2b)
        if not last:
            h_c = h_c + 0.5 * mc[8] * swiglu(modulate(h_c, norm_g[i, 2], mc[6], mc[7]), w1b, w3b, w2b)
    return rmsnorm(h_l, final_g)
```

```python
import functools

import jax
import jax.numpy as jnp
from jax import lax
from jax.experimental import pallas as pl
from jax.experimental.pallas import tpu as pltpu

GRID_W = 64
EPS = 1e-6
N_MOD = 9
HEAD_DIM = 64
D_STATE = 128
N_GROUPS = 8
CHUNK = 128
SSM_CONV_WIDTH = 4

LANES = 128
SUBLANES = 8
COL_TILE = 512
VMEM_LIMIT_BYTES = 56 * 1024 * 1024
MOD_ROWS = 16

_F32 = jnp.float32
_BF16 = jnp.bfloat16
_NEG = -1e30


def _silu(v):
    return v / (1.0 + jnp.exp(-v))


def _dot(a, b):
    return jnp.dot(a, b, preferred_element_type=_F32)


def _modulated(x, g, shift, scale):
    ms = jnp.mean(x * x, axis=-1, keepdims=True)
    return (x * lax.rsqrt(ms + EPS) * g) * (1.0 + scale) + shift


def _params(*sem):
    return pltpu.CompilerParams(dimension_semantics=sem, vmem_limit_bytes=VMEM_LIMIT_BYTES)


def _pick(pref, total):
    t = min(pref, total)
    while total % t:
        t //= 2
    return t


def _mod_kernel(c_ref, w_ref, b_ref, o_ref):
    s = _silu(c_ref[...]).astype(_BF16)
    o_ref[...] = _dot(s, w_ref[...].astype(_BF16)) + b_ref[...]


def _mod_call(c_all, mod_w, mod_b):
    depth, d, nd = mod_w.shape
    tn = _pick(1024, nd)
    return pl.pallas_call(
        _mod_kernel,
        grid=(depth, nd // tn),
        in_specs=[
            pl.BlockSpec((MOD_ROWS, d), lambda l, j: (0, 0)),
            pl.BlockSpec((None, d, tn), lambda l, j: (l, 0, j)),
            pl.BlockSpec((None, 1, tn), lambda l, j: (l, 0, j)),
        ],
        out_specs=pl.BlockSpec((None, MOD_ROWS, tn), lambda l, j: (l, 0, j)),
        out_shape=jax.ShapeDtypeStruct((depth, MOD_ROWS, nd), _F32),
        compiler_params=_params("parallel", "parallel"),
        name="mod",
    )(c_all, mod_w, mod_b.reshape(depth, 1, nd))


def _mod_spec(d, kidx, row0, tiles_per_row):
    return pl.BlockSpec((1, 1, 3 * d), lambda t, *_: (row0 + t // tiles_per_row, 0, kidx))


def _ffn_kernel(x_ref, mod_ref, g_ref, w1_ref, w3_ref, w2_ref, *rest, final_norm):
    if final_norm:
        fg_ref, o_ref, hm_ref = rest
    else:
        o_ref, hm_ref = rest
    j = pl.program_id(1)
    d = x_ref.shape[1]

    @pl.when(j == 0)
    def _():
        m = mod_ref[0]
        hm_ref[...] = _modulated(x_ref[...], g_ref[...], m[:, :d], m[:, d:2 * d]).astype(_BF16)
        o_ref[...] = jnp.zeros_like(o_ref)

    hm = hm_ref[...]
    a = _dot(hm, w1_ref[...])
    b = _dot(hm, w3_ref[...])
    o_ref[...] += _dot((_silu(a) * b).astype(_BF16), w2_ref[...])

    @pl.when(j == pl.num_programs(1) - 1)
    def _():
        m = mod_ref[0]
        h = x_ref[...] + (0.5 * m[:, 2 * d:]) * o_ref[...]
        if final_norm:
            ms = jnp.mean(h * h, axis=-1, keepdims=True)
            h = h * lax.rsqrt(ms + EPS) * fg_ref[...]
        o_ref[...] = h


def _ffn_call(h, mod, g, w1, w3, w2, *, kidx, row0, rows_per_mod, tm_pref, final_g=None):
    t_rows, d = h.shape
    f = w1.shape[1]
    tm = _pick(tm_pref, rows_per_mod)
    tf = _pick(COL_TILE, f)
    in_specs = [
        pl.BlockSpec((tm, d), lambda t, j: (t, 0)),
        _mod_spec(d, kidx, row0, rows_per_mod // tm),
        pl.BlockSpec((1, d), lambda t, j: (0, 0)),
        pl.BlockSpec((d, tf), lambda t, j: (0, j)),
        pl.BlockSpec((d, tf), lambda t, j: (0, j)),
        pl.BlockSpec((tf, d), lambda t, j: (j, 0)),
    ]
    args = [h, mod, g.reshape(1, d), w1, w3, w2]
    if final_g is not None:
        in_specs.append(pl.BlockSpec((1, d), lambda t, j: (0, 0)))
        args.append(final_g.reshape(1, d))
    return pl.pallas_call(
        functools.partial(_ffn_kernel, final_norm=final_g is not None),
        grid=(t_rows // tm, f // tf),
        in_specs=in_specs,
        out_specs=pl.BlockSpec((tm, d), lambda t, j: (t, 0)),
        out_shape=jax.ShapeDtypeStruct((t_rows, d), _F32),
        scratch_shapes=[pltpu.VMEM((tm, d), _BF16)],
        compiler_params=_params("parallel", "arbitrary"),
        name="ffn",
    )(*args)


def _conv_in_kernel(x_ref, mod_ref, g_ref, wa_ref, wag_ref, wbg_ref, wcg_ref, wv_ref,
                    a_ref, cv_ref, bg_ref, hm_ref):
    d = x_ref.shape[1]

    @pl.when(pl.program_id(1) == 0)
    def _():
        m = mod_ref[0]
        hm_ref[...] = _modulated(x_ref[...], g_ref[...], m[:, :d], m[:, d:2 * d]).astype(_BF16)

    hm = hm_ref[...]
    a_ref[...] = _dot(hm, wa_ref[...]) / (1.0 + jnp.exp(-_dot(hm, wag_ref[...])))
    cv_ref[...] = _dot(hm, wcg_ref[...]) * _dot(hm, wv_ref[...])
    bg_ref[...] = _dot(hm, wbg_ref[...])


def _conv_in_call(h, mod, g, in_w, *, row0, rows_per_mod, tm_pref):
    t_rows, d = h.shape
    ca = in_w.shape[1] // 5
    tm = _pick(tm_pref, rows_per_mod)
    tn = _pick(COL_TILE, ca)
    nj = ca // tn

    def wspec(p):
        return pl.BlockSpec((d, tn), lambda t, j: (0, p * nj + j))

    out_spec = pl.BlockSpec((tm, tn), lambda t, j: (t, j))
    out_shape = jax.ShapeDtypeStruct((t_rows, ca), _F32)
    return pl.pallas_call(
        _conv_in_kernel,
        grid=(t_rows // tm, nj),
        in_specs=[
            pl.BlockSpec((tm, d), lambda t, j: (t, 0)),
            _mod_spec(d, 1, row0, rows_per_mod // tm),
            pl.BlockSpec((1, d), lambda t, j: (0, 0)),
            wspec(0), wspec(1), wspec(2), wspec(3), wspec(4),
        ],
        out_specs=[out_spec, out_spec, out_spec],
        out_shape=[out_shape, out_shape, out_shape],
        scratch_shapes=[pltpu.VMEM((tm, d), _BF16)],
        compiler_params=_params("parallel", "arbitrary"),
        name="conv_in",
    )(h, mod, g.reshape(1, d), in_w, in_w, in_w, in_w, in_w)


A_HALO = 16
B_HALO = 8


def _conv_mix_kernel(a_ref, ap_ref, an_ref, cv_ref, cvp_ref, cvn_ref, bg_ref, x_ref, mod_ref,
                     wa_ref, ba_ref, lg_ref, lb_ref, wb_ref, ow_ref, o_ref,
                     aext_ref, cvext_ref, ac_ref, *, tiles_per_seq, ka, kb):
    tm, ca = a_ref.shape
    d = x_ref.shape[1]
    pos = pl.program_id(0) % tiles_per_seq
    has_prev = pos > 0
    has_next = pos < tiles_per_seq - 1

    aext_ref[0:A_HALO, :] = jnp.where(has_prev, ap_ref[...], 0.0)
    aext_ref[A_HALO:A_HALO + tm, :] = a_ref[...]
    aext_ref[A_HALO + tm:, :] = jnp.where(has_next, an_ref[...], 0.0)
    cvext_ref[0:B_HALO, :] = jnp.where(has_prev, cvp_ref[...], 0.0)
    cvext_ref[B_HALO:B_HALO + tm, :] = cv_ref[...]
    cvext_ref[B_HALO + tm:, :] = jnp.where(has_next, cvn_ref[...], 0.0)

    for c in range(ca // LANES):
        cs = slice(c * LANES, (c + 1) * LANES)
        acc = jnp.zeros((tm, LANES), _F32)
        for k in range(ka):
            acc = acc + aext_ref[pl.ds(A_HALO - ka // 2 + k, tm), cs] * wa_ref[k:k + 1, cs]
        ac_ref[:, cs] = acc + ba_ref[:, cs]

    a = ac_ref[...]
    mu = jnp.mean(a, axis=-1, keepdims=True)
    ctr = a - mu
    var = jnp.mean(ctr * ctr, axis=-1, keepdims=True)
    a = _silu(ctr * lax.rsqrt(var + EPS) * lg_ref[...] + lb_ref[...])

    cvc = jnp.zeros((tm, ca), _F32)
    for k in range(kb):
        cvc = cvc + cvext_ref[pl.ds(B_HALO - kb // 2 + k, tm), :] * wb_ref[k:k + 1, :]
    bq = bg_ref[...] * cvc

    mix = _dot(a.astype(_BF16), ow_ref[0:ca, :]) + _dot(bq.astype(_BF16), ow_ref[ca:, :])
    o_ref[...] = x_ref[...] + mod_ref[0][:, 2 * d:] * mix


def _conv_mix_call(a, cv, bg, h, mod, wa, ba, lg, lb, wb, ow, *, row0, rows_per_mod, seq_len):
    t_rows, ca = a.shape
    d = h.shape[1]
    ka, kb = wa.shape[0], wb.shape[0]
    tm = _pick(256, seq_len)
    nha, nhb = tm // A_HALO, tm // B_HALO
    last_a, last_b = t_rows // A_HALO - 1, t_rows // B_HALO - 1
    wa_p = jnp.zeros((32, ca), _F32).at[:ka].set(wa)
    wb_p = jnp.zeros((SUBLANES, ca), _F32).at[:kb].set(wb)
    row = lambda v: v.reshape(1, ca)
    full = lambda r, c: pl.BlockSpec((r, c), lambda t: (0, 0))
    return pl.pallas_call(
        functools.partial(_conv_mix_kernel, tiles_per_seq=seq_len // tm, ka=ka, kb=kb),
        grid=(t_rows // tm,),
        in_specs=[
            pl.BlockSpec((tm, ca), lambda t: (t, 0)),
            pl.BlockSpec((A_HALO, ca), lambda t: (jnp.maximum(t * nha - 1, 0), 0)),
            pl.BlockSpec((A_HALO, ca), lambda t: (jnp.minimum((t + 1) * nha, last_a), 0)),
            pl.BlockSpec((tm, ca), lambda t: (t, 0)),
            pl.BlockSpec((B_HALO, ca), lambda t: (jnp.maximum(t * nhb - 1, 0), 0)),
            pl.BlockSpec((B_HALO, ca), lambda t: (jnp.minimum((t + 1) * nhb, last_b), 0)),
            pl.BlockSpec((tm, ca), lambda t: (t, 0)),
            pl.BlockSpec((tm, d), lambda t: (t, 0)),
            _mod_spec(d, 1, row0, rows_per_mod // tm),
            full(32, ca), full(1, ca), full(1, ca), full(1, ca), full(SUBLANES, ca),
            full(2 * ca, d),
        ],
        out_specs=pl.BlockSpec((tm, d), lambda t: (t, 0)),
        out_shape=jax.ShapeDtypeStruct((t_rows, d), _F32),
        scratch_shapes=[
            pltpu.VMEM((tm + 2 * A_HALO, ca), _F32),
            pltpu.VMEM((tm + 2 * B_HALO, ca), _F32),
            pltpu.VMEM((tm, ca), _F32),
        ],
        compiler_params=_params("parallel"),
        name="conv_mix",
    )(a, a, a, cv, cv, cv, bg, h, mod, wa_p, row(ba), row(lg), row(lb), wb_p, ow)


def _scatter_rows(dst_ref, val, width, col_major):
    if not col_major:
        dst_ref[...] = val
    else:
        for r in range(val.shape[0] // GRID_W):
            dst_ref[:, r * width:(r + 1) * width] = val[r * GRID_W:(r + 1) * GRID_W, :]


def _ssm_in_kernel(x_ref, mod_ref, g_ref, w_ref, wdt_ref, z_ref, xr_ref, bc_ref, dt_ref, hm_ref,
                   *, nz, nx, col_major):
    d = x_ref.shape[1]
    j = pl.program_id(1)

    @pl.when(j == 0)
    def _():
        m = mod_ref[0]
        hm_ref[...] = _modulated(x_ref[...], g_ref[...], m[:, :d], m[:, d:2 * d]).astype(_BF16)
        _scatter_rows(dt_ref, _dot(hm_ref[...], wdt_ref[...]), LANES, col_major)

    u = _dot(hm_ref[...], w_ref[...])

    @pl.when(j < nz)
    def _():
        z_ref[...] = u

    @pl.when(jnp.logical_and(j >= nz, j < nz + nx))
    def _():
        _scatter_rows(xr_ref, u, COL_TILE, col_major)

    @pl.when(j >= nz + nx)
    def _():
        _scatter_rows(bc_ref, u, COL_TILE, col_major)


def _ssm_in_call(h, mod, g, w_main, w_dt, *, di, gn, row0, rows_per_mod, tm_pref, col_major, batch):
    t_rows, d = h.shape
    nz = nx = di // COL_TILE
    nb = 4 * gn // COL_TILE
    tm = _pick(tm_pref, rows_per_mod)
    tps = rows_per_mod // tm
    if col_major:
        rpt = tm // GRID_W
        n_r = rows_per_mod // GRID_W

        def chunk_spec(first, count):
            return pl.BlockSpec(
                (None, None, GRID_W, rpt * COL_TILE),
                lambda t, j: (jnp.clip(j - first, 0, count - 1), t // tps, 0, t % tps))

        xr_spec, bc_spec = chunk_spec(nz, nx), chunk_spec(nz + nx, nb)
        xr_shape = jax.ShapeDtypeStruct((nx, batch, GRID_W, n_r * COL_TILE), _F32)
        bc_shape = jax.ShapeDtypeStruct((nb, batch, GRID_W, n_r * COL_TILE), _F32)
        dt_spec = pl.BlockSpec((None, GRID_W, rpt * LANES), lambda t, j: (t // tps, 0, t % tps))
        dt_shape = jax.ShapeDtypeStruct((batch, GRID_W, n_r * LANES), _F32)
    else:
        def chunk_spec(first, count):
            return pl.BlockSpec((None, tm, COL_TILE),
                                lambda t, j: (jnp.clip(j - first, 0, count - 1), t, 0))

        xr_spec, bc_spec = chunk_spec(nz, nx), chunk_spec(nz + nx, nb)
        xr_shape = jax.ShapeDtypeStruct((nx, t_rows, COL_TILE), _F32)
        bc_shape = jax.ShapeDtypeStruct((nb, t_rows, COL_TILE), _F32)
        dt_spec = pl.BlockSpec((tm, LANES), lambda t, j: (t, 0))
        dt_shape = jax.ShapeDtypeStruct((t_rows, LANES), _F32)
    z, xr, bc, dt = pl.pallas_call(
        functools.partial(_ssm_in_kernel, nz=nz, nx=nx, col_major=col_major),
        grid=(t_rows // tm, nz + nx + nb),
        in_specs=[
            pl.BlockSpec((tm, d), lambda t, j: (t, 0)),
            _mod_spec(d, 1, row0, rows_per_mod // tm),
            pl.BlockSpec((1, d), lambda t, j: (0, 0)),
            pl.BlockSpec((d, COL_TILE), lambda t, j: (0, j)),
            pl.BlockSpec((d, LANES), lambda t, j: (0, 0)),
        ],
        out_specs=[
            pl.BlockSpec((tm, COL_TILE), lambda t, j: (t, jnp.minimum(j, nz - 1))),
            xr_spec, bc_spec, dt_spec,
        ],
        out_shape=[jax.ShapeDtypeStruct((t_rows, di), _F32), xr_shape, bc_shape, dt_shape],
        scratch_shapes=[pltpu.VMEM((tm, d), _BF16)],
        compiler_params=_params("parallel", "arbitrary"),
        name="ssm_in",
    )(h, mod, g.reshape(1, d), w_main, w_dt)
    if col_major:
        xr = xr.reshape(nx, t_rows, COL_TILE)
        bc = bc.reshape(nb, t_rows, COL_TILE)
        dt = dt.reshape(t_rows, LANES)
    return z, xr, bc, dt


def _time_scan(a, reverse):
    q = a.shape[0]
    row = lax.broadcasted_iota(jnp.int32, a.shape, 0)
    sh = 1
    while sh < q:
        if reverse:
            a = a + jnp.where(row < q - sh, pltpu.roll(a, q - sh, 0), 0.0)
        else:
            a = a + jnp.where(row >= sh, pltpu.roll(a, sh, 0), 0.0)
        sh *= 2
    return a


def _ssd_kernel(xr_ref, bc_ref, dt_ref, cw_ref, cb_ref, hp_ref, dsk_ref, h0_ref,
                y_ref, hfin_ref, hst_ref, bx_ref, bb_ref, ext_ref,
                *, reverse, dirn, n_heads, n_groups, nsub):
    q = CHUNK
    kw = SSM_CONV_WIDTH
    n = D_STATE
    gw = (n_heads // n_groups) * HEAD_DIM
    di = n_heads * HEAD_DIM
    gn = n_groups * n
    k = pl.program_id(1)

    @pl.when(k == 0)
    def _():
        hst_ref[...] = h0_ref[...]
        bx_ref[...] = jnp.zeros_like(bx_ref)
        bb_ref[...] = jnp.zeros_like(bb_ref)

    rowi = lax.broadcasted_iota(jnp.int32, (q, q), 0)
    coli = lax.broadcasted_iota(jnp.int32, (q, q), 1)
    tri = (coli >= rowi) if reverse else (coli <= rowi)
    lo_half = lax.broadcasted_iota(jnp.int32, (q, LANES), 1) < HEAD_DIM
    lo_row = lax.broadcasted_iota(jnp.int32, (1, LANES), 1) < HEAD_DIM

    def conv_act(src_ref, ci, r0, c0, width, bnd_ref, wcol):
        cur = src_ref[ci, pl.ds(r0, q), c0:c0 + width]
        bnd = bnd_ref[ci, :, c0:c0 + width]
        acc = jnp.zeros((q, width), _F32) + cb_ref[:, wcol:wcol + width]
        if reverse:
            ext_ref[0:q, 0:width] = cur
            ext_ref[q:q + SUBLANES, 0:width] = bnd
            for t in range(kw):
                acc = acc + ext_ref[pl.ds(kw - 1 - t, q), 0:width] * cw_ref[t:t + 1, wcol:wcol + width]
            bnd_ref[ci, :, c0:c0 + width] = cur[0:SUBLANES, :]
        else:
            ext_ref[0:SUBLANES, 0:width] = bnd
            ext_ref[SUBLANES:SUBLANES + q, 0:width] = cur
            for t in range(kw):
                acc = acc + ext_ref[pl.ds(SUBLANES - (kw - 1) + t, q), 0:width] * cw_ref[t:t + 1, wcol:wcol + width]
            bnd_ref[ci, :, c0:c0 + width] = cur[q - SUBLANES:, :]
        return _silu(acc)

    def chunk_body(ii, carry):
        i = (nsub - 1 - ii) if reverse else ii
        r0 = pl.multiple_of(i * q, q)

        dtv = dt_ref[pl.ds(r0, q), :] + hp_ref[0:1, :]
        dtv = jnp.maximum(dtv, 0.0) + jnp.log(1.0 + jnp.exp(-jnp.abs(dtv)))
        a = dtv * (-jnp.exp(hp_ref[1:2, :]))
        cum = _time_scan(a, reverse)
        cum_t = cum.T
        dt_t = dtv.T
        edge = 0 if reverse else q - 1
        tot = cum[edge:edge + 1, :]
        e_tot = jnp.exp(tot)
        w_t = dt_t * jnp.exp(cum_t[:, edge:edge + 1] - cum_t)

        for g in range(n_groups):
            xg = conv_act(xr_ref, (g * gw) // COL_TILE, r0, (g * gw) % COL_TILE, gw, bx_ref, g * gw)
            b_off, c_off = g * n, gn + g * n
            bg = conv_act(bc_ref, b_off // COL_TILE, r0, b_off % COL_TILE, n, bb_ref, di + b_off)
            cg = conv_act(bc_ref, c_off // COL_TILE, r0, c_off % COL_TILE, n, bb_ref, di + c_off)
            cg16 = cg.astype(_BF16)
            s = lax.dot_general(cg16, bg.astype(_BF16), (((1,), (1,)), ((), ())),
                                preferred_element_type=_F32)
            inter = _dot(cg16, hst_ref[g].astype(_BF16))
            bg_t = bg.T

            for p in range(gw // LANES):
                ls = slice(p * LANES, (p + 1) * LANES)
                xp = xg[:, ls]
                lhs_m, lhs_b, cols = [], [], []
                for hh in range(2):
                    ln = dirn * n_heads + (g * gw) // HEAD_DIM + 2 * p + hh
                    col = jnp.broadcast_to(cum[:, ln:ln + 1], (q, q))
                    seg = col - cum_t[ln:ln + 1, :]
                    lm = jnp.exp(jnp.where(tri, seg, _NEG)) * dt_t[ln:ln + 1, :]
                    lhs_m.append((s * lm).astype(_BF16))
                    lhs_b.append((bg_t * w_t[ln:ln + 1, :]).astype(_BF16))
                    cols.append(col)
                lhs = jnp.concatenate([jnp.concatenate(lhs_m, axis=1),
                                       jnp.concatenate(lhs_b, axis=1)], axis=0)
                rhs = jnp.concatenate([jnp.where(lo_half, xp, 0.0), jnp.where(lo_half, 0.0, xp)],
                                      axis=0).astype(_BF16)
                res = _dot(lhs, rhs)
                decay = jnp.exp(jnp.where(lo_half, cols[0], cols[1]))
                y = res[0:q, :] + decay * inter[:, ls] + dsk_ref[:, g * gw + p * LANES:g * gw + (p + 1) * LANES] * xp
                yc = (g * gw + p * LANES) // COL_TILE
                yl = (g * gw + p * LANES) % COL_TILE
                y_ref[yc, pl.ds(r0, q), yl:yl + LANES] = y
                ln0 = dirn * n_heads + (g * gw) // HEAD_DIM + 2 * p
                e2 = jnp.where(lo_row, jnp.broadcast_to(e_tot[:, ln0:ln0 + 1], (1, LANES)),
                               jnp.broadcast_to(e_tot[:, ln0 + 1:ln0 + 2], (1, LANES)))
                hst_ref[g, :, ls] = hst_ref[g, :, ls] * e2 + res[q:, :]
        return carry

    lax.fori_loop(0, nsub, chunk_body, 0)

    @pl.when(k == pl.num_programs(1) - 1)
    def _():
        hfin_ref[...] = hst_ref[...]


def _ssd_call(xr, bc, dt, cw, cb, hp, dsk, h0, *, reverse, dirn, batch, seq_len, n_heads, n_groups):
    nx, t_rows, _ = xr.shape
    nb = bc.shape[0] // 2
    di = n_heads * HEAD_DIM
    gw = di // n_groups
    rows = _pick(2 * CHUNK, seq_len)
    nblk = seq_len // rows

    def blk(b, k):
        return b * nblk + ((nblk - 1 - k) if reverse else k)

    cwid = cw.shape[1]
    y, hfin = pl.pallas_call(
        functools.partial(_ssd_kernel, reverse=reverse, dirn=dirn, n_heads=n_heads,
                          n_groups=n_groups, nsub=rows // CHUNK),
        grid=(batch, nblk),
        in_specs=[
            pl.BlockSpec((nx, rows, COL_TILE), lambda b, k: (0, blk(b, k), 0)),
            pl.BlockSpec((nb, rows, COL_TILE), lambda b, k: (dirn, blk(b, k), 0)),
            pl.BlockSpec((rows, LANES), lambda b, k: (blk(b, k), 0)),
            pl.BlockSpec((SSM_CONV_WIDTH, cwid), lambda b, k: (0, 0)),
            pl.BlockSpec((1, cwid), lambda b, k: (0, 0)),
            pl.BlockSpec((SUBLANES, LANES), lambda b, k: (0, 0)),
            pl.BlockSpec((1, di), lambda b, k: (0, 0)),
            pl.BlockSpec((None, n_groups, D_STATE, gw), lambda b, k: (b, 0, 0, 0)),
        ],
        out_specs=[
            pl.BlockSpec((nx, rows, COL_TILE), lambda b, k: (0, blk(b, k), 0)),
            pl.BlockSpec((None, n_groups, D_STATE, gw), lambda b, k: (b, 0, 0, 0)),
        ],
        out_shape=[
            jax.ShapeDtypeStruct((nx, t_rows, COL_TILE), _F32),
            jax.ShapeDtypeStruct((batch, n_groups, D_STATE, gw), _F32),
        ],
        scratch_shapes=[
            pltpu.VMEM((n_groups, D_STATE, gw), _F32),
            pltpu.VMEM((nx, SUBLANES, COL_TILE), _F32),
            pltpu.VMEM((nb, SUBLANES, COL_TILE), _F32),
            pltpu.VMEM((CHUNK + SUBLANES, max(gw, D_STATE)), _F32),
        ],
        compiler_params=_params("parallel", "arbitrary"),
        name="ssd",
    )(xr, bc, dt, cw, cb, hp, dsk, h0)
    return y, hfin


def _gather_rows(src_ref, tm, width, col_major):
    if not col_major:
        return src_ref[...]
    return jnp.concatenate(
        [src_ref[:, r * width:(r + 1) * width] for r in range(tm // GRID_W)], axis=0)


def _ssm_out_kernel(yf_ref, yb_ref, z_ref, x_ref, mod_ref, ng_ref, ow_ref, o_ref, *, gw, col_major):
    tm, d = x_ref.shape
    j = pl.program_id(1)

    @pl.when(j == 0)
    def _():
        o_ref[...] = jnp.zeros_like(o_ref)

    y = _gather_rows(yf_ref, tm, COL_TILE, col_major) + _gather_rows(yb_ref, tm, COL_TILE, col_major)
    y = y * _silu(z_ref[...])
    parts = []
    for s in range(COL_TILE // gw):
        ys = y[:, s * gw:(s + 1) * gw]
        parts.append(ys * lax.rsqrt(jnp.mean(ys * ys, axis=-1, keepdims=True) + EPS))
    y = (parts[0] if len(parts) == 1 else jnp.concatenate(parts, axis=1)) * ng_ref[...]
    o_ref[...] += _dot(y.astype(_BF16), ow_ref[...])

    @pl.when(j == pl.num_programs(1) - 1)
    def _():
        o_ref[...] = x_ref[...] + mod_ref[0][:, 2 * d:] * o_ref[...]


def _ssm_out_call(yf, yb, z, h, mod, ng, ow, *, gw, row0, rows_per_mod, tm_pref, col_major, batch):
    t_rows, d = h.shape
    nx = yf.shape[0]
    di = nx * COL_TILE
    tm = _pick(tm_pref, rows_per_mod)
    tps = rows_per_mod // tm
    if col_major:
        rpt = tm // GRID_W
        n_r = rows_per_mod // GRID_W
        yf = yf.reshape(nx, batch, GRID_W, n_r * COL_TILE)
        yb = yb.reshape(nx, batch, GRID_W, n_r * COL_TILE)
        y_spec = pl.BlockSpec((None, None, GRID_W, rpt * COL_TILE),
                              lambda t, j: (j, t // tps, 0, t % tps))
    else:
        y_spec = pl.BlockSpec((None, tm, COL_TILE), lambda t, j: (j, t, 0))
    return pl.pallas_call(
        functools.partial(_ssm_out_kernel, gw=min(gw, COL_TILE), col_major=col_major),
        grid=(t_rows // tm, nx),
        in_specs=[
            y_spec, y_spec,
            pl.BlockSpec((tm, COL_TILE), lambda t, j: (t, j)),
            pl.BlockSpec((tm, d), lambda t, j: (t, 0)),
            _mod_spec(d, 1, row0, rows_per_mod // tm),
            pl.BlockSpec((1, COL_TILE), lambda t, j: (0, j)),
            pl.BlockSpec((COL_TILE, d), lambda t, j: (j, 0)),
        ],
        out_specs=pl.BlockSpec((tm, d), lambda t, j: (t, 0)),
        out_shape=jax.ShapeDtypeStruct((t_rows, d), _F32),
        compiler_params=_params("parallel", "arbitrary"),
        name="ssm_out",
    )(yf, yb, z, h, mod, ng.reshape(1, di), ow)


def kernel(x, c, ctx, c_ctx, mod_w, mod_b, norm_g, ffn_w1, ffn_w3, ffn_w2, conv_in_w, conv_a_w, conv_a_b, conv_ln_g, conv_ln_b, conv_b_w, conv_out_w, ssm_in_w, ssm_conv_w, ssm_conv_b, ssm_dt_bias, ssm_a_log, ssm_d, ssm_norm_g, ssm_out_w, final_g):
    bsz, seq, d = x.shape
    ctx_len = ctx.shape[1]
    depth = mod_w.shape[0]
    di = ssm_out_w.shape[1]
    n_heads = di // HEAD_DIM
    gn = N_GROUPS * D_STATE
    assert bsz + 1 <= MOD_ROWS and 2 * n_heads <= LANES
    assert seq % GRID_W == 0 and di % COL_TILE == 0 and (4 * gn) % COL_TILE == 0

    h_l = x.reshape(bsz * seq, d)
    h_c = ctx.reshape(bsz * ctx_len, d)
    c_all = jnp.zeros((MOD_ROWS, d), _F32).at[:bsz].set(c).at[bsz].set(c_ctx)
    mod_all = _mod_call(c_all, mod_w, mod_b)

    lat = dict(row0=0, rows_per_mod=seq)
    cx = dict(row0=bsz, rows_per_mod=bsz * ctx_len)
    bf = lambda w: w.astype(_BF16)

    for i in range(depth):
        last = i == depth - 1
        mod = mod_all[i].reshape(MOD_ROWS, 1, N_MOD * d)

        w1, w3, w2 = bf(ffn_w1[i, 0]), bf(ffn_w3[i, 0]), bf(ffn_w2[i, 0])
        h_l = _ffn_call(h_l, mod, norm_g[i, 0], w1, w3, w2, kidx=0, tm_pref=512, **lat)
        h_c = _ffn_call(h_c, mod, norm_g[i, 0], w1, w3, w2, kidx=0, tm_pref=512, **cx)

        if i % 2 == 0:
            e = i // 2
            in_w, ow = bf(conv_in_w[e]), bf(conv_out_w[e])
            cp = (conv_a_w[e], conv_a_b[e], conv_ln_g[e], conv_ln_b[e], conv_b_w[e], ow)
            a, cv, bg = _conv_in_call(h_l, mod, norm_g[i, 1], in_w, tm_pref=512, **lat)
            h_l = _conv_mix_call(a, cv, bg, h_l, mod, *cp, seq_len=seq, **lat)
            if not last:
                a, cv, bg = _conv_in_call(h_c, mod, norm_g[i, 1], in_w, tm_pref=512, **cx)
                h_c = _conv_mix_call(a, cv, bg, h_c, mod, *cp, seq_len=ctx_len, **cx)
        else:
            o = i // 2
            col_major = o % 2 == 1
            w_in = ssm_in_w[o]
            dcols = 2 * gn + n_heads
            f0 = 2 * di
            b0 = f0 + dcols
            w_main = bf(jnp.concatenate(
                [w_in[:, :f0], w_in[:, f0:f0 + 2 * gn], w_in[:, b0:b0 + 2 * gn]], axis=1))
            w_dt = jnp.zeros((d, LANES), _F32)
            w_dt = w_dt.at[:, :n_heads].set(w_in[:, f0 + 2 * gn:f0 + dcols])
            w_dt = bf(w_dt.at[:, n_heads:2 * n_heads].set(w_in[:, b0 + 2 * gn:b0 + dcols]))
            hp = jnp.zeros((SUBLANES, LANES), _F32)
            hp = hp.at[0, :2 * n_heads].set(ssm_dt_bias[o].reshape(-1))
            hp = hp.at[1, :2 * n_heads].set(ssm_a_log[o].reshape(-1))
            sargs = dict(di=di, gn=gn, tm_pref=512, batch=bsz)
            z_l, xr_l, bc_l, dt_l = _ssm_in_call(h_l, mod, norm_g[i, 1], w_main, w_dt,
                                                 col_major=col_major, **sargs, **lat)
            z_c, xr_c, bc_c, dt_c = _ssm_in_call(h_c, mod, norm_g[i, 1], w_main, w_dt,
                                                 col_major=False, **sargs, **cx)
            ys_l, ys_c = [], []
            for dirn in range(2):
                cw = ssm_conv_w[o, dirn]
                cb = ssm_conv_b[o, dirn].reshape(1, -1)
                dsk = jnp.repeat(ssm_d[o, dirn], HEAD_DIM).reshape(1, di)
                kw = dict(reverse=dirn == 1, dirn=dirn, batch=bsz, n_heads=n_heads, n_groups=N_GROUPS)
                h0 = jnp.zeros((bsz, N_GROUPS, D_STATE, di // N_GROUPS), _F32)
                y_c, h_fin = _ssd_call(xr_c, bc_c, dt_c, cw, cb, hp, dsk, h0, seq_len=ctx_len, **kw)
                y_l, _ = _ssd_call(xr_l, bc_l, dt_l, cw, cb, hp, dsk, h_fin, seq_len=seq, **kw)
                ys_l.append(y_l)
                ys_c.append(y_c)
            oargs = dict(gw=di // N_GROUPS, tm_pref=512, batch=bsz)
            ow = bf(ssm_out_w[o])
            h_l = _ssm_out_call(ys_l[0], ys_l[1], z_l, h_l, mod, ssm_norm_g[o], ow,
                                col_major=col_major, **oargs, **lat)
            if not last:
                h_c = _ssm_out_call(ys_c[0], ys_c[1], z_c, h_c, mod, ssm_norm_g[o], ow,
                                    col_major=False, **oargs, **cx)

        w1, w3, w2 = bf(ffn_w1[i, 1]), bf(ffn_w3[i, 1]), bf(ffn_w2[i, 1])
        h_l = _ffn_call(h_l, mod, norm_g[i, 2], w1, w3, w2, kidx=2, tm_pref=512,
                        final_g=final_g if last else None, **lat)
        if not last:
            h_c = _ffn_call(h_c, mod, norm_g[i, 2], w1, w3, w2, kidx=2, tm_pref=512, **cx)
    return h_l.reshape(bsz, seq, d)
```

```python
import functools

import jax
import jax.numpy as jnp
from jax import lax
from jax.experimental import pallas as pl
from jax.experimental.pallas import tpu as pltpu

GRID_W = 64
EPS = 1e-6
N_MOD = 9
HEAD_DIM = 64
D_STATE = 128
N_GROUPS = 8
CHUNK = 128
SSM_CONV_WIDTH = 4

LANES = 128
SUBLANES = 8
BF16_ROWS = 16
COL_TILE = 512
VMEM_LIMIT_BYTES = 60 * 1024 * 1024
MOD_ROWS = 16
FFN_TM = 512

_F32 = jnp.float32
_BF16 = jnp.bfloat16
_NEG = -1e30
_LOG2E = 1.4426950408889634


def _silu(v):
    return v / (1.0 + jnp.exp(-v))


def _dot(a, b):
    return jnp.dot(a, b, preferred_element_type=_F32)


def _modulated(x, g, shift, scale):
    ms = jnp.mean(x * x, axis=-1, keepdims=True)
    return (x * lax.rsqrt(ms + EPS) * g) * (1.0 + scale) + shift


def _params(*sem):
    return pltpu.CompilerParams(dimension_semantics=sem, vmem_limit_bytes=VMEM_LIMIT_BYTES)


def _pick(pref, total):
    t = min(pref, total)
    while total % t:
        t //= 2
    return t


def _mod_kernel(c_ref, w_ref, b_ref, o_ref):
    s = _silu(c_ref[...]).astype(_BF16)
    o_ref[...] = _dot(s, w_ref[...].astype(_BF16)) + b_ref[...]


def _mod_call(c_all, mod_w, mod_b):
    depth, d, nd = mod_w.shape
    tn = _pick(1024, nd)
    return pl.pallas_call(
        _mod_kernel,
        grid=(depth, nd // tn),
        in_specs=[
            pl.BlockSpec((MOD_ROWS, d), lambda l, j: (0, 0)),
            pl.BlockSpec((None, d, tn), lambda l, j: (l, 0, j)),
            pl.BlockSpec((None, 1, tn), lambda l, j: (l, 0, j)),
        ],
        out_specs=pl.BlockSpec((None, MOD_ROWS, tn), lambda l, j: (l, 0, j)),
        out_shape=jax.ShapeDtypeStruct((depth, MOD_ROWS, nd), _F32),
        compiler_params=_params("parallel", "parallel"),
        name="mod",
    )(c_all, mod_w, mod_b.reshape(depth, 1, nd))


def _mod_spec(d, kidx, row0, tiles_per_row):
    return pl.BlockSpec((1, 1, 3 * d), lambda t, *_: (row0 + t // tiles_per_row, 0, kidx))


def _ffn_kernel(x_ref, mod_ref, g_ref, w1_ref, w3_ref, w2_ref, *rest, final_norm, perm_nb):
    rest = list(rest)
    fg_ref = rest.pop(0) if final_norm else None
    o_ref, hm_ref = rest[0], rest[1]
    j = pl.program_id(1)
    tm, d = x_ref.shape
    lane_tiles = [slice(c * LANES, (c + 1) * LANES) for c in range(d // LANES)]
    if perm_nb:
        xt_ref, acc_ref = rest[2], rest[3]

    @pl.when(j == 0)
    def _():
        m = mod_ref[0]
        x = x_ref[...]
        hm_ref[...] = _modulated(x, g_ref[...], m[:, :d], m[:, d:2 * d]).astype(_BF16)
        if perm_nb:
            for c, cs in enumerate(lane_tiles):
                xt_ref[c] = x[:, cs]
            acc_ref[...] = jnp.zeros_like(acc_ref)
        else:
            o_ref[...] = jnp.zeros_like(o_ref)

    hm = hm_ref[...]
    a = _dot(hm, w1_ref[...])
    b = _dot(hm, w3_ref[...])
    part = _dot((_silu(a) * b).astype(_BF16), w2_ref[...])
    if perm_nb:
        for c, cs in enumerate(lane_tiles):
            acc_ref[c] += part[:, cs]
    else:
        o_ref[...] += part

    @pl.when(j == pl.num_programs(1) - 1)
    def _():
        gate = 0.5 * mod_ref[0][:, 2 * d:]

        def finish(xv, av):
            h = xv + gate * av
            if final_norm:
                ms = jnp.mean(h * h, axis=-1, keepdims=True)
                h = h * lax.rsqrt(ms + EPS) * fg_ref[...]
            return h

        if perm_nb:
            na = tm // perm_nb
            for bi in range(perm_nb):
                rows = pl.ds(bi, na, stride=perm_nb)
                gather = lambda ref: jnp.concatenate(
                    [ref[c, rows, :] for c in range(len(lane_tiles))], axis=1)
                o_ref[bi] = finish(gather(xt_ref), gather(acc_ref))
        else:
            o_ref[...] = finish(x_ref[...], o_ref[...])


def _ffn_call(h, mod, g, w1, w3, w2, *, kidx, row0, rows_per_mod, tm_pref, final_g=None, perm_nb=0):
    t_rows, d = h.shape
    f = w1.shape[1]
    tm = _pick(tm_pref, rows_per_mod)
    tf = _pick(COL_TILE, f)
    tps = rows_per_mod // tm
    in_specs = [
        pl.BlockSpec((tm, d), lambda t, j: (t, 0)),
        _mod_spec(d, kidx, row0, tps),
        pl.BlockSpec((1, d), lambda t, j: (0, 0)),
        pl.BlockSpec((d, tf), lambda t, j: (0, j)),
        pl.BlockSpec((d, tf), lambda t, j: (0, j)),
        pl.BlockSpec((tf, d), lambda t, j: (j, 0)),
    ]
    args = [h, mod, g.reshape(1, d), w1, w3, w2]
    if final_g is not None:
        in_specs.append(pl.BlockSpec((1, d), lambda t, j: (0, 0)))
        args.append(final_g.reshape(1, d))
    scratch = [pltpu.VMEM((tm, d), _BF16)]
    if perm_nb:
        na = tm // perm_nb
        assert tm % perm_nb == 0 and na % SUBLANES == 0
        out_spec = pl.BlockSpec((None, perm_nb, na, d), lambda t, j: (t // tps, 0, t % tps, 0))
        out_shape = jax.ShapeDtypeStruct(
            (t_rows // rows_per_mod, perm_nb, rows_per_mod // perm_nb, d), _F32)
        scratch += [pltpu.VMEM((d // LANES, tm, LANES), _F32)] * 2
    else:
        out_spec = pl.BlockSpec((tm, d), lambda t, j: (t, 0))
        out_shape = jax.ShapeDtypeStruct((t_rows, d), _F32)
    out = pl.pallas_call(
        functools.partial(_ffn_kernel, final_norm=final_g is not None, perm_nb=perm_nb),
        grid=(t_rows // tm, f // tf),
        in_specs=in_specs,
        out_specs=out_spec,
        out_shape=out_shape,
        scratch_shapes=scratch,
        compiler_params=_params("parallel", "arbitrary"),
        name="ffn",
    )(*args)
    return out.reshape(t_rows, d)


def _conv_in_kernel(x_ref, mod_ref, g_ref, wa_ref, wag_ref, wbg_ref, wcg_ref, wv_ref,
                    a_ref, cv_ref, bg_ref, hm_ref):
    d = x_ref.shape[1]

    @pl.when(pl.program_id(1) == 0)
    def _():
        m = mod_ref[0]
        hm_ref[...] = _modulated(x_ref[...], g_ref[...], m[:, :d], m[:, d:2 * d]).astype(_BF16)

    hm = hm_ref[...]
    a_ref[...] = _dot(hm, wa_ref[...]) / (1.0 + jnp.exp(-_dot(hm, wag_ref[...])))
    cv_ref[...] = _dot(hm, wcg_ref[...]) * _dot(hm, wv_ref[...])
    bg_ref[...] = _dot(hm, wbg_ref[...])


def _conv_in_call(h, mod, g, in_w, *, row0, rows_per_mod, tm_pref):
    t_rows, d = h.shape
    ca = in_w.shape[1] // 5
    tm = _pick(tm_pref, rows_per_mod)
    tn = _pick(COL_TILE, ca)
    nj = ca // tn

    def wspec(p):
        return pl.BlockSpec((d, tn), lambda t, j: (0, p * nj + j))

    out_spec = pl.BlockSpec((tm, tn), lambda t, j: (t, j))
    out_shape = jax.ShapeDtypeStruct((t_rows, ca), _F32)
    return pl.pallas_call(
        _conv_in_kernel,
        grid=(t_rows // tm, nj),
        in_specs=[
            pl.BlockSpec((tm, d), lambda t, j: (t, 0)),
            _mod_spec(d, 1, row0, rows_per_mod // tm),
            pl.BlockSpec((1, d), lambda t, j: (0, 0)),
            wspec(0), wspec(1), wspec(2), wspec(3), wspec(4),
        ],
        out_specs=[out_spec, out_spec, out_spec],
        out_shape=[out_shape, out_shape, out_shape],
        scratch_shapes=[pltpu.VMEM((tm, d), _BF16)],
        compiler_params=_params("parallel", "arbitrary"),
        name="conv_in",
    )(h, mod, g.reshape(1, d), in_w, in_w, in_w, in_w, in_w)


A_HALO = 16
B_HALO = 8


def _conv_mix_kernel(a_ref, ap_ref, an_ref, cv_ref, cvp_ref, cvn_ref, bg_ref, x_ref, mod_ref,
                     wa_ref, ba_ref, lg_ref, lb_ref, wb_ref, ow_ref, o_ref,
                     aext_ref, cvext_ref, ac_ref, *, tiles_per_seq, ka, kb):
    tm, ca = a_ref.shape
    d = x_ref.shape[1]
    pos = pl.program_id(0) % tiles_per_seq
    has_prev = pos > 0
    has_next = pos < tiles_per_seq - 1

    aext_ref[0:A_HALO, :] = jnp.where(has_prev, ap_ref[...], 0.0)
    aext_ref[A_HALO:A_HALO + tm, :] = a_ref[...]
    aext_ref[A_HALO + tm:, :] = jnp.where(has_next, an_ref[...], 0.0)
    cvext_ref[0:B_HALO, :] = jnp.where(has_prev, cvp_ref[...], 0.0)
    cvext_ref[B_HALO:B_HALO + tm, :] = cv_ref[...]
    cvext_ref[B_HALO + tm:, :] = jnp.where(has_next, cvn_ref[...], 0.0)

    n_ext = tm + 2 * A_HALO
    for c in range(ca // LANES):
        cs = slice(c * LANES, (c + 1) * LANES)
        ext = aext_ref[:, cs]
        phases = [ext] + [pltpu.roll(ext, n_ext - r, 0) for r in range(1, SUBLANES)]
        acc = jnp.zeros((tm, LANES), _F32) + ba_ref[:, cs]
        for k in range(ka):
            off = A_HALO - ka // 2 + k
            base = off - off % SUBLANES
            acc = acc + phases[off % SUBLANES][base:base + tm, :] * wa_ref[k:k + 1, cs]
        ac_ref[:, cs] = acc

    a = ac_ref[...]
    mu = jnp.mean(a, axis=-1, keepdims=True)
    ctr = a - mu
    var = jnp.mean(ctr * ctr, axis=-1, keepdims=True)
    a = _silu(ctr * lax.rsqrt(var + EPS) * lg_ref[...] + lb_ref[...])

    cvc = jnp.zeros((tm, ca), _F32)
    for k in range(kb):
        cvc = cvc + cvext_ref[pl.ds(B_HALO - kb // 2 + k, tm), :] * wb_ref[k:k + 1, :]
    bq = bg_ref[...] * cvc

    mix = _dot(a.astype(_BF16), ow_ref[0:ca, :]) + _dot(bq.astype(_BF16), ow_ref[ca:, :])
    o_ref[...] = x_ref[...] + mod_ref[0][:, 2 * d:] * mix


def _conv_mix_call(a, cv, bg, h, mod, wa, ba, lg, lb, wb, ow, *, row0, rows_per_mod, seq_len):
    t_rows, ca = a.shape
    d = h.shape[1]
    ka, kb = wa.shape[0], wb.shape[0]
    tm = _pick(256, seq_len)
    nha, nhb = tm // A_HALO, tm // B_HALO
    last_a, last_b = t_rows // A_HALO - 1, t_rows // B_HALO - 1
    wa_p = jnp.zeros((32, ca), _F32).at[:ka].set(wa)
    wb_p = jnp.zeros((SUBLANES, ca), _F32).at[:kb].set(wb)
    row = lambda v: v.reshape(1, ca)
    full = lambda r, c: pl.BlockSpec((r, c), lambda t: (0, 0))
    return pl.pallas_call(
        functools.partial(_conv_mix_kernel, tiles_per_seq=seq_len // tm, ka=ka, kb=kb),
        grid=(t_rows // tm,),
        in_specs=[
            pl.BlockSpec((tm, ca), lambda t: (t, 0)),
            pl.BlockSpec((A_HALO, ca), lambda t: (jnp.maximum(t * nha - 1, 0), 0)),
            pl.BlockSpec((A_HALO, ca), lambda t: (jnp.minimum((t + 1) * nha, last_a), 0)),
            pl.BlockSpec((tm, ca), lambda t: (t, 0)),
            pl.BlockSpec((B_HALO, ca), lambda t: (jnp.maximum(t * nhb - 1, 0), 0)),
            pl.BlockSpec((B_HALO, ca), lambda t: (jnp.minimum((t + 1) * nhb, last_b), 0)),
            pl.BlockSpec((tm, ca), lambda t: (t, 0)),
            pl.BlockSpec((tm, d), lambda t: (t, 0)),
            _mod_spec(d, 1, row0, rows_per_mod // tm),
            full(32, ca), full(1, ca), full(1, ca), full(1, ca), full(SUBLANES, ca),
            full(2 * ca, d),
        ],
        out_specs=pl.BlockSpec((tm, d), lambda t: (t, 0)),
        out_shape=jax.ShapeDtypeStruct((t_rows, d), _F32),
        scratch_shapes=[
            pltpu.VMEM((tm + 2 * A_HALO, ca), _F32),
            pltpu.VMEM((tm + 2 * B_HALO, ca), _F32),
            pltpu.VMEM((tm, ca), _F32),
        ],
        compiler_params=_params("parallel"),
        name="conv_mix",
    )(a, a, a, cv, cv, cv, bg, h, mod, wa_p, row(ba), row(lg), row(lb), wb_p, ow)


S_HALO = BF16_ROWS
TAP_ROWS = 16
BWD_TAPS = 8
SSM_IN_ROW_BLOCK = 256


def _ssm_in_kernel(x_ref, xp_ref, xn_ref, mod_ref, g_ref, w_ref, wdt_ref, cw_ref,
                   z_ref, xs_ref, bc_ref, dt_ref, hm_ref, *, nz, nx, nbd, tiles_per_seq):
    tm, d = x_ref.shape
    kw = SSM_CONV_WIDTH
    j = pl.program_id(1)
    pos = pl.program_id(0) % tiles_per_seq
    mid = slice(S_HALO, S_HALO + tm)

    @pl.when(j == 0)
    def _():
        m = mod_ref[0]
        mod = lambda v: _modulated(v, g_ref[...], m[:, :d], m[:, d:2 * d]).astype(_BF16)
        hm_ref[0:S_HALO, :] = mod(xp_ref[...])
        hm_ref[mid, :] = mod(x_ref[...])
        hm_ref[S_HALO + tm:, :] = mod(xn_ref[...])
        dt_ref[...] = _dot(hm_ref[mid, :], wdt_ref[...])

    rb = min(SSM_IN_ROW_BLOCK, tm)
    n_ext = rb + 2 * S_HALO

    def conv_act(dsts):
        for lo in range(0, tm, rb):
            u = _dot(hm_ref[lo:lo + n_ext, :], w_ref[...])
            if lo == 0:
                u = jnp.concatenate([jnp.where(pos > 0, u[0:S_HALO, :], 0.0), u[S_HALO:, :]], axis=0)
            if lo + rb == tm:
                u = jnp.concatenate([u[0:S_HALO + rb, :],
                                     jnp.where(pos < tiles_per_seq - 1, u[S_HALO + rb:, :], 0.0)], axis=0)
            for c in range(COL_TILE // LANES):
                cs = slice(c * LANES, (c + 1) * LANES)
                ext = u[:, cs]
                phases = {0: ext}
                for dst_ref, reverse in dsts:
                    base = BWD_TAPS if reverse else 0
                    acc = jnp.zeros((rb, LANES), _F32) + cw_ref[base + kw:base + kw + 1, cs]
                    for t in range(kw):
                        off = S_HALO + (kw - 1 - t) if reverse else S_HALO - (kw - 1) + t
                        r = off % SUBLANES
                        if r not in phases:
                            phases[r] = pltpu.roll(ext, n_ext - r, 0)
                        acc = acc + phases[r][off - r:off - r + rb, :] * cw_ref[base + t:base + t + 1, cs]
                    dst_ref[lo:lo + rb, cs] = _silu(acc).astype(_BF16)

    @pl.when(j < nz)
    def _():
        z_ref[...] = _dot(hm_ref[mid, :], w_ref[...]).astype(_BF16)

    @pl.when(jnp.logical_and(j >= nz, j < nz + nx))
    def _():
        conv_act([(xs_ref.at[0], False), (xs_ref.at[1], True)])

    @pl.when(jnp.logical_and(j >= nz + nx, j < nz + nx + nbd))
    def _():
        conv_act([(bc_ref, False)])

    @pl.when(j >= nz + nx + nbd)
    def _():
        conv_act([(bc_ref, True)])


def _ssm_in_call(h, mod, g, w_main, w_dt, cw_tbl, *, di, gn, row0, rows_per_mod, seq_len, tm_pref):
    t_rows, d = h.shape
    nz = nx = di // COL_TILE
    nbd = 2 * gn // COL_TILE
    tm = _pick(tm_pref, seq_len)
    nh = tm // S_HALO
    last_h = t_rows // S_HALO - 1
    z, xs, bc, dt = pl.pallas_call(
        functools.partial(_ssm_in_kernel, nz=nz, nx=nx, nbd=nbd, tiles_per_seq=seq_len // tm),
        grid=(t_rows // tm, nz + nx + 2 * nbd),
        in_specs=[
            pl.BlockSpec((tm, d), lambda t, j: (t, 0)),
            pl.BlockSpec((S_HALO, d), lambda t, j: (jnp.maximum(t * nh - 1, 0), 0)),
            pl.BlockSpec((S_HALO, d), lambda t, j: (jnp.minimum((t + 1) * nh, last_h), 0)),
            _mod_spec(d, 1, row0, rows_per_mod // tm),
            pl.BlockSpec((1, d), lambda t, j: (0, 0)),
            pl.BlockSpec((d, COL_TILE), lambda t, j: (0, j)),
            pl.BlockSpec((d, LANES), lambda t, j: (0, 0)),
            pl.BlockSpec((None, TAP_ROWS, COL_TILE), lambda t, j: (j, 0, 0)),
        ],
        out_specs=[
            pl.BlockSpec((tm, COL_TILE), lambda t, j: (t, jnp.minimum(j, nz - 1))),
            pl.BlockSpec((2, None, tm, COL_TILE), lambda t, j: (0, jnp.clip(j - nz, 0, nx - 1), t, 0)),
            pl.BlockSpec((None, tm, COL_TILE),
                         lambda t, j: (jnp.clip(j - nz - nx, 0, 2 * nbd - 1), t, 0)),
            pl.BlockSpec((tm, LANES), lambda t, j: (t, 0)),
        ],
        out_shape=[
            jax.ShapeDtypeStruct((t_rows, di), _BF16),
            jax.ShapeDtypeStruct((2, nx, t_rows, COL_TILE), _BF16),
            jax.ShapeDtypeStruct((2 * nbd, t_rows, COL_TILE), _BF16),
            jax.ShapeDtypeStruct((t_rows, LANES), _F32),
        ],
        scratch_shapes=[pltpu.VMEM((tm + 2 * S_HALO, d), _BF16)],
        compiler_params=_params("parallel", "arbitrary"),
        name="ssm_in",
    )(h, h, h, mod, g.reshape(1, d), w_main, w_dt, cw_tbl)
    return z, xs, bc, dt


def _time_scan(a, reverse):
    q = a.shape[0]
    row = lax.broadcasted_iota(jnp.int32, a.shape, 0)
    sh = 1
    while sh < q:
        if reverse:
            a = a + jnp.where(row < q - sh, pltpu.roll(a, q - sh, 0), 0.0)
        else:
            a = a + jnp.where(row >= sh, pltpu.roll(a, sh, 0), 0.0)
        sh *= 2
    return a


def _ssd_kernel(xs_ref, bc_ref, dt_ref, hp_ref, dsk_ref, h0_ref, y_ref, hfin_ref, hst_ref,
                *, reverse, dirn, n_heads, n_groups, nsub):
    q = CHUNK
    n = D_STATE
    gw = (n_heads // n_groups) * HEAD_DIM
    gn = n_groups * n
    k = pl.program_id(1)

    @pl.when(k == 0)
    def _():
        hst_ref[...] = h0_ref[...]

    rowi = lax.broadcasted_iota(jnp.int32, (q, q), 0)
    coli = lax.broadcasted_iota(jnp.int32, (q, q), 1)
    tri = (coli >= rowi) if reverse else (coli <= rowi)
    lo_half = lax.broadcasted_iota(jnp.int32, (q, LANES), 1) < HEAD_DIM
    lo_row = lax.broadcasted_iota(jnp.int32, (1, LANES), 1) < HEAD_DIM
    edge = 0 if reverse else q - 1

    def chunk_body(ii, carry):
        i = (nsub - 1 - ii) if reverse else ii
        r0 = pl.multiple_of(i * q, q)
        rows = pl.ds(r0, q)

        dtv = dt_ref[rows, :] + hp_ref[0:1, :]
        dtv = jnp.maximum(dtv, 0.0) + jnp.log(1.0 + jnp.exp(-jnp.abs(dtv)))
        c2 = _time_scan(dtv * (-_LOG2E * jnp.exp(hp_ref[1:2, :])), reverse)
        c2_t = c2.T
        lq_t = c2_t - jnp.log2(dtv.T)
        e_tot = jnp.exp2(c2[edge:edge + 1, :])
        w_t = jnp.exp2(c2_t[:, edge:edge + 1] - lq_t)

        for g in range(n_groups):
            x0 = g * gw
            xg = xs_ref[x0 // COL_TILE, rows, x0 % COL_TILE:x0 % COL_TILE + gw]
            b_off, c_off = g * n, gn + g * n
            bg = bc_ref[b_off // COL_TILE, rows, b_off % COL_TILE:b_off % COL_TILE + n]
            cg = bc_ref[c_off // COL_TILE, rows, c_off % COL_TILE:c_off % COL_TILE + n]
            s = lax.dot_general(cg, bg, (((1,), (1,)), ((), ())),
                                preferred_element_type=_F32)
            inter = _dot(cg, hst_ref[g].astype(_BF16))
            bg_t = bg.astype(_F32).T

            for p in range(gw // LANES):
                ls = slice(p * LANES, (p + 1) * LANES)
                xp = xg[:, ls].astype(_F32)
                ln0 = dirn * n_heads + x0 // HEAD_DIM + 2 * p
                lhs_m, lhs_b, cols = [], [], []
                for ln in (ln0, ln0 + 1):
                    col = jnp.broadcast_to(c2[:, ln:ln + 1], (q, q))
                    lm = jnp.exp2(jnp.where(tri, col - lq_t[ln:ln + 1, :], _NEG))
                    lhs_m.append((s * lm).astype(_BF16))
                    lhs_b.append((bg_t * w_t[ln:ln + 1, :]).astype(_BF16))
                    cols.append(col)
                lhs = jnp.concatenate([jnp.concatenate(lhs_m, axis=1),
                                       jnp.concatenate(lhs_b, axis=1)], axis=0)
                rhs = jnp.concatenate([jnp.where(lo_half, xp, 0.0), jnp.where(lo_half, 0.0, xp)],
                                      axis=0).astype(_BF16)
                res = _dot(lhs, rhs)
                decay = jnp.exp2(jnp.where(lo_half, cols[0], cols[1]))
                y = res[0:q, :] + decay * inter[:, ls] + dsk_ref[:, x0 + p * LANES:x0 + (p + 1) * LANES] * xp
                yl = (x0 + p * LANES) % COL_TILE
                y_ref[(x0 + p * LANES) // COL_TILE, rows, yl:yl + LANES] = y.astype(_BF16)
                e2 = jnp.where(lo_row, jnp.broadcast_to(e_tot[:, ln0:ln0 + 1], (1, LANES)),
                               jnp.broadcast_to(e_tot[:, ln0 + 1:ln0 + 2], (1, LANES)))
                hst_ref[g, :, ls] = hst_ref[g, :, ls] * e2 + res[q:, :]
        return carry

    lax.fori_loop(0, nsub, chunk_body, 0)

    @pl.when(k == pl.num_programs(1) - 1)
    def _():
        hfin_ref[...] = hst_ref[...]


def _ssd_call(xs, bc, dt, hp, dsk, h0, *, reverse, dirn, batch, seq_len, n_heads, n_groups):
    _, nx, t_rows, _ = xs.shape
    nbd = bc.shape[0] // 2
    di = n_heads * HEAD_DIM
    gw = di // n_groups
    rows = _pick(4 * CHUNK, seq_len)
    nblk = seq_len // rows

    def blk(b, k):
        return b * nblk + ((nblk - 1 - k) if reverse else k)

    state_spec = pl.BlockSpec((None, n_groups, D_STATE, gw), lambda b, k: (b, 0, 0, 0))
    y, hfin = pl.pallas_call(
        functools.partial(_ssd_kernel, reverse=reverse, dirn=dirn, n_heads=n_heads,
                          n_groups=n_groups, nsub=rows // CHUNK),
        grid=(batch, nblk),
        in_specs=[
            pl.BlockSpec((None, nx, rows, COL_TILE), lambda b, k: (dirn, 0, blk(b, k), 0)),
            pl.BlockSpec((nbd, rows, COL_TILE), lambda b, k: (dirn, blk(b, k), 0)),
            pl.BlockSpec((rows, LANES), lambda b, k: (blk(b, k), 0)),
            pl.BlockSpec((SUBLANES, LANES), lambda b, k: (0, 0)),
            pl.BlockSpec((1, di), lambda b, k: (0, 0)),
            state_spec,
        ],
        out_specs=[
            pl.BlockSpec((nx, rows, COL_TILE), lambda b, k: (0, blk(b, k), 0)),
            state_spec,
        ],
        out_shape=[
            jax.ShapeDtypeStruct((nx, t_rows, COL_TILE), _BF16),
            jax.ShapeDtypeStruct((batch, n_groups, D_STATE, gw), _F32),
        ],
        scratch_shapes=[pltpu.VMEM((n_groups, D_STATE, gw), _F32)],
        compiler_params=_params("parallel", "arbitrary"),
        name="ssd",
    )(xs, bc, dt, hp, dsk, h0)
    return y, hfin


def _ssm_out_kernel(yf_ref, yb_ref, z_ref, x_ref, mod_ref, ng_ref, ow_ref, o_ref, *, gw):
    d = x_ref.shape[1]
    j = pl.program_id(1)

    @pl.when(j == 0)
    def _():
        o_ref[...] = jnp.zeros_like(o_ref)

    y = yf_ref[...].astype(_F32) + yb_ref[...].astype(_F32)
    y = y * _silu(z_ref[...].astype(_F32))
    parts = []
    for s in range(COL_TILE // gw):
        ys = y[:, s * gw:(s + 1) * gw]
        parts.append(ys * lax.rsqrt(jnp.mean(ys * ys, axis=-1, keepdims=True) + EPS))
    y = (parts[0] if len(parts) == 1 else jnp.concatenate(parts, axis=1)) * ng_ref[...]
    o_ref[...] += _dot(y.astype(_BF16), ow_ref[...])

    @pl.when(j == pl.num_programs(1) - 1)
    def _():
        o_ref[...] = x_ref[...] + mod_ref[0][:, 2 * d:] * o_ref[...]


def _ssm_out_call(yf, yb, z, h, mod, ng, ow, *, gw, row0, rows_per_mod, tm_pref):
    t_rows, d = h.shape
    nx = yf.shape[0]
    di = nx * COL_TILE
    tm = _pick(tm_pref, rows_per_mod)
    y_spec = pl.BlockSpec((None, tm, COL_TILE), lambda t, j: (j, t, 0))
    return pl.pallas_call(
        functools.partial(_ssm_out_kernel, gw=min(gw, COL_TILE)),
        grid=(t_rows // tm, nx),
        in_specs=[
            y_spec, y_spec,
            pl.BlockSpec((tm, COL_TILE), lambda t, j: (t, j)),
            pl.BlockSpec((tm, d), lambda t, j: (t, 0)),
            _mod_spec(d, 1, row0, rows_per_mod // tm),
            pl.BlockSpec((1, COL_TILE), lambda t, j: (0, j)),
            pl.BlockSpec((COL_TILE, d), lambda t, j: (j, 0)),
        ],
        out_specs=pl.BlockSpec((tm, d), lambda t, j: (t, 0)),
        out_shape=jax.ShapeDtypeStruct((t_rows, d), _F32),
        compiler_params=_params("parallel", "arbitrary"),
        name="ssm_out",
    )(yf, yb, z, h, mod, ng.reshape(1, di), ow)


def _ssm_weights(w_in, conv_w, conv_b, dt_bias, a_log, *, di, gn, n_heads):
    d = w_in.shape[0]
    dcols = 2 * gn + n_heads
    f0 = 2 * di
    b0 = f0 + dcols
    w_main = jnp.concatenate(
        [w_in[:, :f0], w_in[:, f0:f0 + 2 * gn], w_in[:, b0:b0 + 2 * gn]], axis=1).astype(_BF16)
    w_dt = jnp.zeros((d, LANES), _F32)
    w_dt = w_dt.at[:, :n_heads].set(w_in[:, f0 + 2 * gn:f0 + dcols])
    w_dt = w_dt.at[:, n_heads:2 * n_heads].set(w_in[:, b0 + 2 * gn:b0 + dcols]).astype(_BF16)
    hp = jnp.zeros((SUBLANES, LANES), _F32)
    hp = hp.at[0, :2 * n_heads].set(dt_bias.reshape(-1))
    hp = hp.at[1, :2 * n_heads].set(a_log.reshape(-1))

    kw = SSM_CONV_WIDTH
    half = jnp.zeros((BWD_TAPS, COL_TILE), _F32)

    def taps(dirn, lo):
        t = half.at[:kw].set(conv_w[dirn][:, lo:lo + COL_TILE])
        return t.at[kw].set(conv_b[dirn][lo:lo + COL_TILE])

    nz = nx = di // COL_TILE
    nbd = 2 * gn // COL_TILE
    blocks = [jnp.concatenate([half, half])] * nz
    blocks += [jnp.concatenate([taps(0, c * COL_TILE), taps(1, c * COL_TILE)]) for c in range(nx)]
    blocks += [jnp.concatenate([taps(0, di + c * COL_TILE), half]) for c in range(nbd)]
    blocks += [jnp.concatenate([half, taps(1, di + c * COL_TILE)]) for c in range(nbd)]
    return w_main, w_dt, hp, jnp.stack(blocks)


def kernel(x, c, ctx, c_ctx, mod_w, mod_b, norm_g, ffn_w1, ffn_w3, ffn_w2, conv_in_w, conv_a_w, conv_a_b, conv_ln_g, conv_ln_b, conv_b_w, conv_out_w, ssm_in_w, ssm_conv_w, ssm_conv_b, ssm_dt_bias, ssm_a_log, ssm_d, ssm_norm_g, ssm_out_w, final_g):
    bsz, seq, d = x.shape
    ctx_len = ctx.shape[1]
    depth = mod_w.shape[0]
    di = ssm_out_w.shape[1]
    n_heads = di // HEAD_DIM
    gn = N_GROUPS * D_STATE
    n_rows = seq // GRID_W
    assert bsz + 1 <= MOD_ROWS and 2 * n_heads <= LANES
    assert seq % GRID_W == 0 and di % COL_TILE == 0 and (2 * gn) % COL_TILE == 0

    h_l = x.reshape(bsz * seq, d)
    h_c = ctx.reshape(bsz * ctx_len, d)
    c_all = jnp.zeros((MOD_ROWS, d), _F32).at[:bsz].set(c).at[bsz].set(c_ctx)
    mod_all = _mod_call(c_all, mod_w, mod_b)

    lat = dict(row0=0, rows_per_mod=seq)
    cx = dict(row0=bsz, rows_per_mod=bsz * ctx_len)
    bf = lambda w: w.astype(_BF16)

    def col_major(i):
        return i < depth and i % 2 == 1 and (i // 2) % 2 == 1

    assert not col_major(0)
    cur_col = False
    for i in range(depth):
        last = i == depth - 1
        mod = mod_all[i].reshape(MOD_ROWS, 1, N_MOD * d)
        assert cur_col == col_major(i)

        w1, w3, w2 = bf(ffn_w1[i, 0]), bf(ffn_w3[i, 0]), bf(ffn_w2[i, 0])
        h_l = _ffn_call(h_l, mod, norm_g[i, 0], w1, w3, w2, kidx=0, tm_pref=FFN_TM, **lat)
        h_c = _ffn_call(h_c, mod, norm_g[i, 0], w1, w3, w2, kidx=0, tm_pref=FFN_TM, **cx)

        if i % 2 == 0:
            e = i // 2
            in_w, ow = bf(conv_in_w[e]), bf(conv_out_w[e])
            cp = (conv_a_w[e], conv_a_b[e], conv_ln_g[e], conv_ln_b[e], conv_b_w[e], ow)
            a, cv, bg = _conv_in_call(h_l, mod, norm_g[i, 1], in_w, tm_pref=512, **lat)
            h_l = _conv_mix_call(a, cv, bg, h_l, mod, *cp, seq_len=seq, **lat)
            if not last:
                a, cv, bg = _conv_in_call(h_c, mod, norm_g[i, 1], in_w, tm_pref=512, **cx)
                h_c = _conv_mix_call(a, cv, bg, h_c, mod, *cp, seq_len=ctx_len, **cx)
        else:
            o = i // 2
            w_main, w_dt, hp, cw_tbl = _ssm_weights(
                ssm_in_w[o], ssm_conv_w[o], ssm_conv_b[o], ssm_dt_bias[o], ssm_a_log[o],
                di=di, gn=gn, n_heads=n_heads)
            sargs = dict(di=di, gn=gn, tm_pref=1024)
            z_l, xs_l, bc_l, dt_l = _ssm_in_call(h_l, mod, norm_g[i, 1], w_main, w_dt, cw_tbl,
                                                 seq_len=seq, **sargs, **lat)
            z_c, xs_c, bc_c, dt_c = _ssm_in_call(h_c, mod, norm_g[i, 1], w_main, w_dt, cw_tbl,
                                                 seq_len=ctx_len, **sargs, **cx)
            ys_l, ys_c = [], []
            for dirn in range(2):
                dsk = jnp.repeat(ssm_d[o, dirn], HEAD_DIM).reshape(1, di)
                kw = dict(reverse=dirn == 1, dirn=dirn, batch=bsz, n_heads=n_heads, n_groups=N_GROUPS)
                h0 = jnp.zeros((bsz, N_GROUPS, D_STATE, di // N_GROUPS), _F32)
                y_c, h_fin = _ssd_call(xs_c, bc_c, dt_c, hp, dsk, h0, seq_len=ctx_len, **kw)
                y_l, _ = _ssd_call(xs_l, bc_l, dt_l, hp, dsk, h_fin, seq_len=seq, **kw)
                ys_l.append(y_l)
                ys_c.append(y_c)
            oargs = dict(gw=di // N_GROUPS, tm_pref=1024)
            ow = bf(ssm_out_w[o])
            h_l = _ssm_out_call(ys_l[0], ys_l[1], z_l, h_l, mod, ssm_norm_g[o], ow, **oargs, **lat)
            if not last:
                h_c = _ssm_out_call(ys_c[0], ys_c[1], z_c, h_c, mod, ssm_norm_g[o], ow, **oargs, **cx)

        w1, w3, w2 = bf(ffn_w1[i, 1]), bf(ffn_w3[i, 1]), bf(ffn_w2[i, 1])
        flip = cur_col != col_major(i + 1)
        perm_nb = (n_rows if cur_col else GRID_W) if flip else 0
        h_l = _ffn_call(h_l, mod, norm_g[i, 2], w1, w3, w2, kidx=2, tm_pref=FFN_TM,
                        final_g=final_g if last else None, perm_nb=perm_nb, **lat)
        cur_col = cur_col != flip
        if not last:
            h_c = _ffn_call(h_c, mod, norm_g[i, 2], w1, w3, w2, kidx=2, tm_pref=FFN_TM, **cx)
    return h_l.reshape(bsz, seq, d)
```

```python
import functools

import jax
import jax.numpy as jnp
from jax import lax
from jax.experimental import pallas as pl
from jax.experimental.pallas import tpu as pltpu

GRID_W = 64
EPS = 1e-6
N_MOD = 9
HEAD_DIM = 64
D_STATE = 128
N_GROUPS = 8
CHUNK = 128
SSM_CONV_WIDTH = 4

LANES = 128
SUBLANES = 8
BF16_ROWS = 16
COL_TILE = 512
VMEM_LIMIT_BYTES = 60 * 1024 * 1024
MOD_ROWS = 16
PERM_PAD = 8
FFN_TM = 512

_F32 = jnp.float32
_BF16 = jnp.bfloat16
_NEG = -1e30
_LOG2E = 1.4426950408889634


def _silu(v):
    return v / (1.0 + jnp.exp(-v))


def _dot(a, b):
    return jnp.dot(a, b, preferred_element_type=_F32)


def _modulated(x, g, shift, scale):
    ms = jnp.mean(x * x, axis=-1, keepdims=True)
    return (x * lax.rsqrt(ms + EPS) * g) * (1.0 + scale) + shift


def _params(*sem):
    return pltpu.CompilerParams(dimension_semantics=sem, vmem_limit_bytes=VMEM_LIMIT_BYTES)


def _pick(pref, total):
    t = min(pref, total)
    while total % t:
        t //= 2
    return t


def _mod_kernel(c_ref, w_ref, b_ref, o_ref):
    s = _silu(c_ref[...]).astype(_BF16)
    o_ref[...] = _dot(s, w_ref[...].astype(_BF16)) + b_ref[...]


def _mod_call(c_all, mod_w, mod_b):
    depth, d, nd = mod_w.shape
    tn = _pick(1024, nd)
    return pl.pallas_call(
        _mod_kernel,
        grid=(depth, nd // tn),
        in_specs=[
            pl.BlockSpec((MOD_ROWS, d), lambda l, j: (0, 0)),
            pl.BlockSpec((None, d, tn), lambda l, j: (l, 0, j)),
            pl.BlockSpec((None, 1, tn), lambda l, j: (l, 0, j)),
        ],
        out_specs=pl.BlockSpec((None, MOD_ROWS, tn), lambda l, j: (l, 0, j)),
        out_shape=jax.ShapeDtypeStruct((depth, MOD_ROWS, nd), _F32),
        compiler_params=_params("parallel", "parallel"),
        name="mod",
    )(c_all, mod_w, mod_b.reshape(depth, 1, nd))


def _mod_spec(d, kidx, row0, tiles_per_row):
    return pl.BlockSpec((1, 1, 3 * d), lambda t, *_: (row0 + t // tiles_per_row, 0, kidx))


def _ffn_kernel(x_ref, mod_ref, g_ref, w1_ref, w3_ref, w2_ref, *rest, final_norm, perm_nb):
    rest = list(rest)
    fg_ref = rest.pop(0) if final_norm else None
    o_ref, hm_ref = rest[0], rest[1]
    j = pl.program_id(1)
    tm, d = x_ref.shape
    lane_tiles = [slice(c * LANES, (c + 1) * LANES) for c in range(d // LANES)]
    if perm_nb:
        xt_ref, acc_ref = rest[2], rest[3]
        na = tm // perm_nb
        pitch = perm_nb + PERM_PAD
        groups = [(slice(a * perm_nb, (a + 1) * perm_nb), slice(a * pitch, a * pitch + perm_nb))
                  for a in range(na)]

    @pl.when(j == 0)
    def _():
        m = mod_ref[0]
        x = x_ref[...]
        hm_ref[...] = _modulated(x, g_ref[...], m[:, :d], m[:, d:2 * d]).astype(_BF16)
        if perm_nb:
            for c, cs in enumerate(lane_tiles):
                for src, dst in groups:
                    xt_ref[c, dst, :] = x[src, cs]
            acc_ref[...] = jnp.zeros_like(acc_ref)
        else:
            o_ref[...] = jnp.zeros_like(o_ref)

    hm = hm_ref[...]
    a = _dot(hm, w1_ref[...])
    b = _dot(hm, w3_ref[...])
    part = _dot((_silu(a) * b).astype(_BF16), w2_ref[...])
    if perm_nb:
        for c, cs in enumerate(lane_tiles):
            for src, dst in groups:
                acc_ref[c, dst, :] += part[src, cs]
    else:
        o_ref[...] += part

    @pl.when(j == pl.num_programs(1) - 1)
    def _():
        gate = 0.5 * mod_ref[0][:, 2 * d:]

        def finish(xv, av):
            h = xv + gate * av
            if final_norm:
                ms = jnp.mean(h * h, axis=-1, keepdims=True)
                h = h * lax.rsqrt(ms + EPS) * fg_ref[...]
            return h

        if perm_nb:
            for bi in range(perm_nb):
                rows = pl.ds(bi, na, stride=pitch)
                gather = lambda ref: jnp.concatenate(
                    [ref[c, rows, :] for c in range(len(lane_tiles))], axis=1)
                o_ref[bi] = finish(gather(xt_ref), gather(acc_ref))
        else:
            o_ref[...] = finish(x_ref[...], o_ref[...])


def _ffn_call(h, mod, g, w1, w3, w2, *, kidx, row0, rows_per_mod, tm_pref, final_g=None, perm_nb=0):
    t_rows, d = h.shape
    f = w1.shape[1]
    tm = _pick(tm_pref, rows_per_mod)
    tf = _pick(COL_TILE, f)
    tps = rows_per_mod // tm
    in_specs = [
        pl.BlockSpec((tm, d), lambda t, j: (t, 0)),
        _mod_spec(d, kidx, row0, tps),
        pl.BlockSpec((1, d), lambda t, j: (0, 0)),
        pl.BlockSpec((d, tf), lambda t, j: (0, j)),
        pl.BlockSpec((d, tf), lambda t, j: (0, j)),
        pl.BlockSpec((tf, d), lambda t, j: (j, 0)),
    ]
    args = [h, mod, g.reshape(1, d), w1, w3, w2]
    if final_g is not None:
        in_specs.append(pl.BlockSpec((1, d), lambda t, j: (0, 0)))
        args.append(final_g.reshape(1, d))
    scratch = [pltpu.VMEM((tm, d), _BF16)]
    if perm_nb:
        na = tm // perm_nb
        assert tm % perm_nb == 0 and na % SUBLANES == 0
        out_spec = pl.BlockSpec((None, perm_nb, na, d), lambda t, j: (t // tps, 0, t % tps, 0))
        out_shape = jax.ShapeDtypeStruct(
            (t_rows // rows_per_mod, perm_nb, rows_per_mod // perm_nb, d), _F32)
        scratch += [pltpu.VMEM((d // LANES, na * (perm_nb + PERM_PAD), LANES), _F32)] * 2
    else:
        out_spec = pl.BlockSpec((tm, d), lambda t, j: (t, 0))
        out_shape = jax.ShapeDtypeStruct((t_rows, d), _F32)
    out = pl.pallas_call(
        functools.partial(_ffn_kernel, final_norm=final_g is not None, perm_nb=perm_nb),
        grid=(t_rows // tm, f // tf),
        in_specs=in_specs,
        out_specs=out_spec,
        out_shape=out_shape,
        scratch_shapes=scratch,
        compiler_params=_params("parallel", "arbitrary"),
        name="ffn",
    )(*args)
    return out.reshape(t_rows, d)


def _conv_in_kernel(x_ref, mod_ref, g_ref, wa_ref, wag_ref, wbg_ref, wcg_ref, wv_ref,
                    a_ref, cv_ref, bg_ref, hm_ref):
    d = x_ref.shape[1]

    @pl.when(pl.program_id(1) == 0)
    def _():
        m = mod_ref[0]
        hm_ref[...] = _modulated(x_ref[...], g_ref[...], m[:, :d], m[:, d:2 * d]).astype(_BF16)

    hm = hm_ref[...]
    a_ref[...] = _dot(hm, wa_ref[...]) / (1.0 + jnp.exp(-_dot(hm, wag_ref[...])))
    cv_ref[...] = _dot(hm, wcg_ref[...]) * _dot(hm, wv_ref[...])
    bg_ref[...] = _dot(hm, wbg_ref[...])


def _conv_in_call(h, mod, g, in_w, *, row0, rows_per_mod, tm_pref):
    t_rows, d = h.shape
    ca = in_w.shape[1] // 5
    tm = _pick(tm_pref, rows_per_mod)
    tn = _pick(COL_TILE, ca)
    nj = ca // tn

    def wspec(p):
        return pl.BlockSpec((d, tn), lambda t, j: (0, p * nj + j))

    out_spec = pl.BlockSpec((tm, tn), lambda t, j: (t, j))
    out_shape = jax.ShapeDtypeStruct((t_rows, ca), _F32)
    return pl.pallas_call(
        _conv_in_kernel,
        grid=(t_rows // tm, nj),
        in_specs=[
            pl.BlockSpec((tm, d), lambda t, j: (t, 0)),
            _mod_spec(d, 1, row0, rows_per_mod // tm),
            pl.BlockSpec((1, d), lambda t, j: (0, 0)),
            wspec(0), wspec(1), wspec(2), wspec(3), wspec(4),
        ],
        out_specs=[out_spec, out_spec, out_spec],
        out_shape=[out_shape, out_shape, out_shape],
        scratch_shapes=[pltpu.VMEM((tm, d), _BF16)],
        compiler_params=_params("parallel", "arbitrary"),
        name="conv_in",
    )(h, mod, g.reshape(1, d), in_w, in_w, in_w, in_w, in_w)


A_HALO = 16
B_HALO = 8


def _conv_mix_kernel(a_ref, ap_ref, an_ref, cv_ref, cvp_ref, cvn_ref, bg_ref, x_ref, mod_ref,
                     wa_ref, ba_ref, lg_ref, lb_ref, wb_ref, ow_ref, o_ref,
                     aext_ref, cvext_ref, ac_ref, *, tiles_per_seq, ka, kb):
    tm, ca = a_ref.shape
    d = x_ref.shape[1]
    pos = pl.program_id(0) % tiles_per_seq
    has_prev = pos > 0
    has_next = pos < tiles_per_seq - 1

    aext_ref[0:A_HALO, :] = jnp.where(has_prev, ap_ref[...], 0.0)
    aext_ref[A_HALO:A_HALO + tm, :] = a_ref[...]
    aext_ref[A_HALO + tm:, :] = jnp.where(has_next, an_ref[...], 0.0)
    cvext_ref[0:B_HALO, :] = jnp.where(has_prev, cvp_ref[...], 0.0)
    cvext_ref[B_HALO:B_HALO + tm, :] = cv_ref[...]
    cvext_ref[B_HALO + tm:, :] = jnp.where(has_next, cvn_ref[...], 0.0)

    n_ext = tm + 2 * A_HALO
    for c in range(ca // LANES):
        cs = slice(c * LANES, (c + 1) * LANES)
        ext = aext_ref[:, cs]
        phases = [ext] + [pltpu.roll(ext, n_ext - r, 0) for r in range(1, SUBLANES)]
        acc = jnp.zeros((tm, LANES), _F32) + ba_ref[:, cs]
        for k in range(ka):
            off = A_HALO - ka // 2 + k
            base = off - off % SUBLANES
            acc = acc + phases[off % SUBLANES][base:base + tm, :] * wa_ref[k:k + 1, cs]
        ac_ref[:, cs] = acc

    a = ac_ref[...]
    mu = jnp.mean(a, axis=-1, keepdims=True)
    ctr = a - mu
    var = jnp.mean(ctr * ctr, axis=-1, keepdims=True)
    a = _silu(ctr * lax.rsqrt(var + EPS) * lg_ref[...] + lb_ref[...])

    cvc = jnp.zeros((tm, ca), _F32)
    for k in range(kb):
        cvc = cvc + cvext_ref[pl.ds(B_HALO - kb // 2 + k, tm), :] * wb_ref[k:k + 1, :]
    bq = bg_ref[...] * cvc

    mix = _dot(a.astype(_BF16), ow_ref[0:ca, :]) + _dot(bq.astype(_BF16), ow_ref[ca:, :])
    o_ref[...] = x_ref[...] + mod_ref[0][:, 2 * d:] * mix


def _conv_mix_call(a, cv, bg, h, mod, wa, ba, lg, lb, wb, ow, *, row0, rows_per_mod, seq_len):
    t_rows, ca = a.shape
    d = h.shape[1]
    ka, kb = wa.shape[0], wb.shape[0]
    tm = _pick(256, seq_len)
    nha, nhb = tm // A_HALO, tm // B_HALO
    last_a, last_b = t_rows // A_HALO - 1, t_rows // B_HALO - 1
    wa_p = jnp.zeros((32, ca), _F32).at[:ka].set(wa)
    wb_p = jnp.zeros((SUBLANES, ca), _F32).at[:kb].set(wb)
    row = lambda v: v.reshape(1, ca)
    full = lambda r, c: pl.BlockSpec((r, c), lambda t: (0, 0))
    return pl.pallas_call(
        functools.partial(_conv_mix_kernel, tiles_per_seq=seq_len // tm, ka=ka, kb=kb),
        grid=(t_rows // tm,),
        in_specs=[
            pl.BlockSpec((tm, ca), lambda t: (t, 0)),
            pl.BlockSpec((A_HALO, ca), lambda t: (jnp.maximum(t * nha - 1, 0), 0)),
            pl.BlockSpec((A_HALO, ca), lambda t: (jnp.minimum((t + 1) * nha, last_a), 0)),
            pl.BlockSpec((tm, ca), lambda t: (t, 0)),
            pl.BlockSpec((B_HALO, ca), lambda t: (jnp.maximum(t * nhb - 1, 0), 0)),
            pl.BlockSpec((B_HALO, ca), lambda t: (jnp.minimum((t + 1) * nhb, last_b), 0)),
            pl.BlockSpec((tm, ca), lambda t: (t, 0)),
            pl.BlockSpec((tm, d), lambda t: (t, 0)),
            _mod_spec(d, 1, row0, rows_per_mod // tm),
            full(32, ca), full(1, ca), full(1, ca), full(1, ca), full(SUBLANES, ca),
            full(2 * ca, d),
        ],
        out_specs=pl.BlockSpec((tm, d), lambda t: (t, 0)),
        out_shape=jax.ShapeDtypeStruct((t_rows, d), _F32),
        scratch_shapes=[
            pltpu.VMEM((tm + 2 * A_HALO, ca), _F32),
            pltpu.VMEM((tm + 2 * B_HALO, ca), _F32),
            pltpu.VMEM((tm, ca), _F32),
        ],
        compiler_params=_params("parallel"),
        name="conv_mix",
    )(a, a, a, cv, cv, cv, bg, h, mod, wa_p, row(ba), row(lg), row(lb), wb_p, ow)


S_HALO = BF16_ROWS
TAP_ROWS = 16
BWD_TAPS = 8
SSM_IN_ROW_BLOCK = 1024


def _ssm_in_kernel(x_ref, xp_ref, xn_ref, mod_ref, g_ref, w_ref, wdt_ref, cw_ref,
                   z_ref, xs_ref, bc_ref, dt_ref, hm_ref, *, nz, nx, nbd, tiles_per_seq):
    tm, d = x_ref.shape
    kw = SSM_CONV_WIDTH
    j = pl.program_id(1)
    pos = pl.program_id(0) % tiles_per_seq
    mid = slice(S_HALO, S_HALO + tm)

    @pl.when(j == 0)
    def _():
        m = mod_ref[0]
        mod = lambda v: _modulated(v, g_ref[...], m[:, :d], m[:, d:2 * d]).astype(_BF16)
        hm_ref[0:S_HALO, :] = mod(xp_ref[...])
        hm_ref[mid, :] = mod(x_ref[...])
        hm_ref[S_HALO + tm:, :] = mod(xn_ref[...])
        dt_ref[...] = _dot(hm_ref[mid, :], wdt_ref[...])

    rb = min(SSM_IN_ROW_BLOCK, tm)
    n_ext = rb + 2 * S_HALO

    def conv_act(dsts):
        for lo in range(0, tm, rb):
            u = _dot(hm_ref[lo:lo + n_ext, :], w_ref[...])
            if lo == 0:
                u = jnp.concatenate([jnp.where(pos > 0, u[0:S_HALO, :], 0.0), u[S_HALO:, :]], axis=0)
            if lo + rb == tm:
                u = jnp.concatenate([u[0:S_HALO + rb, :],
                                     jnp.where(pos < tiles_per_seq - 1, u[S_HALO + rb:, :], 0.0)], axis=0)
            for c in range(COL_TILE // LANES):
                cs = slice(c * LANES, (c + 1) * LANES)
                ext = u[:, cs]
                phases = {0: ext}
                for dst_ref, reverse in dsts:
                    base = BWD_TAPS if reverse else 0
                    acc = jnp.zeros((rb, LANES), _F32) + cw_ref[base + kw:base + kw + 1, cs]
                    for t in range(kw):
                        off = S_HALO + (kw - 1 - t) if reverse else S_HALO - (kw - 1) + t
                        r = off % SUBLANES
                        if r not in phases:
                            phases[r] = pltpu.roll(ext, n_ext - r, 0)
                        acc = acc + phases[r][off - r:off - r + rb, :] * cw_ref[base + t:base + t + 1, cs]
                    dst_ref[lo:lo + rb, cs] = _silu(acc).astype(_BF16)

    @pl.when(j < nz)
    def _():
        z_ref[...] = _dot(hm_ref[mid, :], w_ref[...]).astype(_BF16)

    @pl.when(jnp.logical_and(j >= nz, j < nz + nx))
    def _():
        conv_act([(xs_ref.at[0], False), (xs_ref.at[1], True)])

    @pl.when(jnp.logical_and(j >= nz + nx, j < nz + nx + nbd))
    def _():
        conv_act([(bc_ref, False)])

    @pl.when(j >= nz + nx + nbd)
    def _():
        conv_act([(bc_ref, True)])


def _ssm_in_call(h, mod, g, w_main, w_dt, cw_tbl, *, di, gn, row0, rows_per_mod, seq_len, tm_pref):
    t_rows, d = h.shape
    nz = nx = di // COL_TILE
    nbd = 2 * gn // COL_TILE
    tm = _pick(tm_pref, seq_len)
    nh = tm // S_HALO
    last_h = t_rows // S_HALO - 1
    z, xs, bc, dt = pl.pallas_call(
        functools.partial(_ssm_in_kernel, nz=nz, nx=nx, nbd=nbd, tiles_per_seq=seq_len // tm),
        grid=(t_rows // tm, nz + nx + 2 * nbd),
        in_specs=[
            pl.BlockSpec((tm, d), lambda t, j: (t, 0)),
            pl.BlockSpec((S_HALO, d), lambda t, j: (jnp.maximum(t * nh - 1, 0), 0)),
            pl.BlockSpec((S_HALO, d), lambda t, j: (jnp.minimum((t + 1) * nh, last_h), 0)),
            _mod_spec(d, 1, row0, rows_per_mod // tm),
            pl.BlockSpec((1, d), lambda t, j: (0, 0)),
            pl.BlockSpec((d, COL_TILE), lambda t, j: (0, j)),
            pl.BlockSpec((d, LANES), lambda t, j: (0, 0)),
            pl.BlockSpec((None, TAP_ROWS, COL_TILE), lambda t, j: (j, 0, 0)),
        ],
        out_specs=[
            pl.BlockSpec((tm, COL_TILE), lambda t, j: (t, jnp.minimum(j, nz - 1))),
            pl.BlockSpec((2, None, tm, COL_TILE), lambda t, j: (0, jnp.clip(j - nz, 0, nx - 1), t, 0)),
            pl.BlockSpec((None, tm, COL_TILE),
                         lambda t, j: (jnp.clip(j - nz - nx, 0, 2 * nbd - 1), t, 0)),
            pl.BlockSpec((tm, LANES), lambda t, j: (t, 0)),
        ],
        out_shape=[
            jax.ShapeDtypeStruct((t_rows, di), _BF16),
            jax.ShapeDtypeStruct((2, nx, t_rows, COL_TILE), _BF16),
            jax.ShapeDtypeStruct((2 * nbd, t_rows, COL_TILE), _BF16),
            jax.ShapeDtypeStruct((t_rows, LANES), _F32),
        ],
        scratch_shapes=[pltpu.VMEM((tm + 2 * S_HALO, d), _BF16)],
        compiler_params=_params("parallel", "arbitrary"),
        name="ssm_in",
    )(h, h, h, mod, g.reshape(1, d), w_main, w_dt, cw_tbl)
    return z, xs, bc, dt


def _time_scan(a, reverse):
    q = a.shape[0]
    row = lax.broadcasted_iota(jnp.int32, a.shape, 0)
    sh = 1
    while sh < q:
        if reverse:
            a = a + jnp.where(row < q - sh, pltpu.roll(a, q - sh, 0), 0.0)
        else:
            a = a + jnp.where(row >= sh, pltpu.roll(a, sh, 0), 0.0)
        sh *= 2
    return a


def _ssd_kernel(xs_ref, bc_ref, dt_ref, hp_ref, dsk_ref, h0_ref, y_ref, hfin_ref, hst_ref,
                *, reverse, dirn, n_heads, n_groups, nsub):
    q = CHUNK
    n = D_STATE
    gw = (n_heads // n_groups) * HEAD_DIM
    gn = n_groups * n
    k = pl.program_id(1)

    @pl.when(k == 0)
    def _():
        hst_ref[...] = h0_ref[...]

    rowi = lax.broadcasted_iota(jnp.int32, (q, q), 0)
    coli = lax.broadcasted_iota(jnp.int32, (q, q), 1)
    tri = (coli >= rowi) if reverse else (coli <= rowi)
    lo_half = lax.broadcasted_iota(jnp.int32, (q, LANES), 1) < HEAD_DIM
    lo_row = lax.broadcasted_iota(jnp.int32, (1, LANES), 1) < HEAD_DIM
    sel_lo = jnp.where(lo_half, 1.0, 0.0).astype(_BF16)
    sel_hi = jnp.where(lo_half, 0.0, 1.0).astype(_BF16)
    edge = 0 if reverse else q - 1

    def chunk_body(ii, carry):
        i = (nsub - 1 - ii) if reverse else ii
        r0 = pl.multiple_of(i * q, q)
        rows = pl.ds(r0, q)

        dtv = dt_ref[rows, :] + hp_ref[0:1, :]
        dtv = jnp.maximum(dtv, 0.0) + jnp.log(1.0 + jnp.exp(-jnp.abs(dtv)))
        c2 = _time_scan(dtv * (-_LOG2E * jnp.exp(hp_ref[1:2, :])), reverse)
        c2_t = c2.T
        lq_t = c2_t - jnp.log2(dtv.T)
        e_tot = jnp.exp2(c2[edge:edge + 1, :])
        w_t = jnp.exp2(c2_t[:, edge:edge + 1] - lq_t)

        for g in range(n_groups):
            x0 = g * gw
            xg = xs_ref[x0 // COL_TILE, rows, x0 % COL_TILE:x0 % COL_TILE + gw]
            b_off, c_off = g * n, gn + g * n
            bg = bc_ref[b_off // COL_TILE, rows, b_off % COL_TILE:b_off % COL_TILE + n]
            cg = bc_ref[c_off // COL_TILE, rows, c_off % COL_TILE:c_off % COL_TILE + n]
            s = lax.dot_general(cg, bg, (((1,), (1,)), ((), ())),
                                preferred_element_type=_F32)
            inter = _dot(cg, hst_ref[g].astype(_BF16))
            bg_t = bg.astype(_F32).T.astype(_BF16)

            for p in range(gw // LANES):
                ls = slice(p * LANES, (p + 1) * LANES)
                xp = xg[:, ls]
                ln0 = dirn * n_heads + x0 // HEAD_DIM + 2 * p
                lhs_m, lhs_b, cols = [], [], []
                for ln in (ln0, ln0 + 1):
                    col = jnp.broadcast_to(c2[:, ln:ln + 1], (q, q))
                    lm = jnp.exp2(jnp.where(tri, col - lq_t[ln:ln + 1, :], _NEG))
                    lhs_m.append((s * lm).astype(_BF16))
                    lhs_b.append(bg_t * jnp.broadcast_to(w_t[ln:ln + 1, :], (n, q)).astype(_BF16))
                    cols.append(col)
                lhs = jnp.concatenate([jnp.concatenate(lhs_m, axis=1),
                                       jnp.concatenate(lhs_b, axis=1)], axis=0)
                rhs = jnp.concatenate([xp * sel_lo, xp * sel_hi], axis=0)
                res = _dot(lhs, rhs)
                decay = jnp.exp2(jnp.where(lo_half, cols[0], cols[1]))
                y = (res[0:q, :] + decay * inter[:, ls]
                     + dsk_ref[:, x0 + p * LANES:x0 + (p + 1) * LANES] * xp.astype(_F32))
                yl = (x0 + p * LANES) % COL_TILE
                y_ref[(x0 + p * LANES) // COL_TILE, rows, yl:yl + LANES] = y.astype(_BF16)
                e2 = jnp.where(lo_row, jnp.broadcast_to(e_tot[:, ln0:ln0 + 1], (1, LANES)),
                               jnp.broadcast_to(e_tot[:, ln0 + 1:ln0 + 2], (1, LANES)))
                hst_ref[g, :, ls] = hst_ref[g, :, ls] * e2 + res[q:, :]
        return carry

    lax.fori_loop(0, nsub, chunk_body, 0)

    @pl.when(k == pl.num_programs(1) - 1)
    def _():
        hfin_ref[...] = hst_ref[...]


def _ssd_call(xs, bc, dt, hp, dsk, h0, *, reverse, dirn, batch, seq_len, n_heads, n_groups):
    _, nx, t_rows, _ = xs.shape
    nbd = bc.shape[0] // 2
    di = n_heads * HEAD_DIM
    gw = di // n_groups
    rows = _pick(4 * CHUNK, seq_len)
    nblk = seq_len // rows

    def blk(b, k):
        return b * nblk + ((nblk - 1 - k) if reverse else k)

    state_spec = pl.BlockSpec((None, n_groups, D_STATE, gw), lambda b, k: (b, 0, 0, 0))
    y, hfin = pl.pallas_call(
        functools.partial(_ssd_kernel, reverse=reverse, dirn=dirn, n_heads=n_heads,
                          n_groups=n_groups, nsub=rows // CHUNK),
        grid=(batch, nblk),
        in_specs=[
            pl.BlockSpec((None, nx, rows, COL_TILE), lambda b, k: (dirn, 0, blk(b, k), 0)),
            pl.BlockSpec((nbd, rows, COL_TILE), lambda b, k: (dirn, blk(b, k), 0)),
            pl.BlockSpec((rows, LANES), lambda b, k: (blk(b, k), 0)),
            pl.BlockSpec((SUBLANES, LANES), lambda b, k: (0, 0)),
            pl.BlockSpec((1, di), lambda b, k: (0, 0)),
            state_spec,
        ],
        out_specs=[
            pl.BlockSpec((nx, rows, COL_TILE), lambda b, k: (0, blk(b, k), 0)),
            state_spec,
        ],
        out_shape=[
            jax.ShapeDtypeStruct((nx, t_rows, COL_TILE), _BF16),
            jax.ShapeDtypeStruct((batch, n_groups, D_STATE, gw), _F32),
        ],
        scratch_shapes=[pltpu.VMEM((n_groups, D_STATE, gw), _F32)],
        compiler_params=_params("parallel", "arbitrary"),
        name="ssd",
    )(xs, bc, dt, hp, dsk, h0)
    return y, hfin


def _ssm_out_kernel(yf_ref, yb_ref, z_ref, x_ref, mod_ref, ng_ref, ow_ref, o_ref, *, gw):
    d = x_ref.shape[1]
    j = pl.program_id(1)

    @pl.when(j == 0)
    def _():
        o_ref[...] = jnp.zeros_like(o_ref)

    y = yf_ref[...].astype(_F32) + yb_ref[...].astype(_F32)
    y = y * _silu(z_ref[...].astype(_F32))
    parts = []
    for s in range(COL_TILE // gw):
        ys = y[:, s * gw:(s + 1) * gw]
        parts.append(ys * lax.rsqrt(jnp.mean(ys * ys, axis=-1, keepdims=True) + EPS))
    y = (parts[0] if len(parts) == 1 else jnp.concatenate(parts, axis=1)) * ng_ref[...]
    o_ref[...] += _dot(y.astype(_BF16), ow_ref[...])

    @pl.when(j == pl.num_programs(1) - 1)
    def _():
        o_ref[...] = x_ref[...] + mod_ref[0][:, 2 * d:] * o_ref[...]


def _ssm_out_call(yf, yb, z, h, mod, ng, ow, *, gw, row0, rows_per_mod, tm_pref):
    t_rows, d = h.shape
    nx = yf.shape[0]
    di = nx * COL_TILE
    tm = _pick(tm_pref, rows_per_mod)
    y_spec = pl.BlockSpec((None, tm, COL_TILE), lambda t, j: (j, t, 0))
    return pl.pallas_call(
        functools.partial(_ssm_out_kernel, gw=min(gw, COL_TILE)),
        grid=(t_rows // tm, nx),
        in_specs=[
            y_spec, y_spec,
            pl.BlockSpec((tm, COL_TILE), lambda t, j: (t, j)),
            pl.BlockSpec((tm, d), lambda t, j: (t, 0)),
            _mod_spec(d, 1, row0, rows_per_mod // tm),
            pl.BlockSpec((1, COL_TILE), lambda t, j: (0, j)),
            pl.BlockSpec((COL_TILE, d), lambda t, j: (j, 0)),
        ],
        out_specs=pl.BlockSpec((tm, d), lambda t, j: (t, 0)),
        out_shape=jax.ShapeDtypeStruct((t_rows, d), _F32),
        compiler_params=_params("parallel", "arbitrary"),
        name="ssm_out",
    )(yf, yb, z, h, mod, ng.reshape(1, di), ow)


def _ssm_weights(w_in, conv_w, conv_b, dt_bias, a_log, *, di, gn, n_heads):
    d = w_in.shape[0]
    dcols = 2 * gn + n_heads
    f0 = 2 * di
    b0 = f0 + dcols
    w_main = jnp.concatenate(
        [w_in[:, :f0], w_in[:, f0:f0 + 2 * gn], w_in[:, b0:b0 + 2 * gn]], axis=1).astype(_BF16)
    w_dt = jnp.zeros((d, LANES), _F32)
    w_dt = w_dt.at[:, :n_heads].set(w_in[:, f0 + 2 * gn:f0 + dcols])
    w_dt = w_dt.at[:, n_heads:2 * n_heads].set(w_in[:, b0 + 2 * gn:b0 + dcols]).astype(_BF16)
    hp = jnp.zeros((SUBLANES, LANES), _F32)
    hp = hp.at[0, :2 * n_heads].set(dt_bias.reshape(-1))
    hp = hp.at[1, :2 * n_heads].set(a_log.reshape(-1))

    kw = SSM_CONV_WIDTH
    half = jnp.zeros((BWD_TAPS, COL_TILE), _F32)

    def taps(dirn, lo):
        t = half.at[:kw].set(conv_w[dirn][:, lo:lo + COL_TILE])
        return t.at[kw].set(conv_b[dirn][lo:lo + COL_TILE])

    nz = nx = di // COL_TILE
    nbd = 2 * gn // COL_TILE
    blocks = [jnp.concatenate([half, half])] * nz
    blocks += [jnp.concatenate([taps(0, c * COL_TILE), taps(1, c * COL_TILE)]) for c in range(nx)]
    blocks += [jnp.concatenate([taps(0, di + c * COL_TILE), half]) for c in range(nbd)]
    blocks += [jnp.concatenate([half, taps(1, di + c * COL_TILE)]) for c in range(nbd)]
    return w_main, w_dt, hp, jnp.stack(blocks)


def kernel(x, c, ctx, c_ctx, mod_w, mod_b, norm_g, ffn_w1, ffn_w3, ffn_w2, conv_in_w, conv_a_w, conv_a_b, conv_ln_g, conv_ln_b, conv_b_w, conv_out_w, ssm_in_w, ssm_conv_w, ssm_conv_b, ssm_dt_bias, ssm_a_log, ssm_d, ssm_norm_g, ssm_out_w, final_g):
    bsz, seq, d = x.shape
    ctx_len = ctx.shape[1]
    depth = mod_w.shape[0]
    di = ssm_out_w.shape[1]
    n_heads = di // HEAD_DIM
    gn = N_GROUPS * D_STATE
    n_rows = seq // GRID_W
    assert bsz + 1 <= MOD_ROWS and 2 * n_heads <= LANES
    assert seq % GRID_W == 0 and di % COL_TILE == 0 and (2 * gn) % COL_TILE == 0

    h_l = x.reshape(bsz * seq, d)
    h_c = ctx.reshape(bsz * ctx_len, d)
    c_all = jnp.zeros((MOD_ROWS, d), _F32).at[:bsz].set(c).at[bsz].set(c_ctx)
    mod_all = _mod_call(c_all, mod_w, mod_b)

    lat = dict(row0=0, rows_per_mod=seq)
    cx = dict(row0=bsz, rows_per_mod=bsz * ctx_len)
    bf = lambda w: w.astype(_BF16)

    def col_major(i):
        return i < depth and i % 2 == 1 and (i // 2) % 2 == 1

    assert not col_major(0)
    cur_col = False
    for i in range(depth):
        last = i == depth - 1
        mod = mod_all[i].reshape(MOD_ROWS, 1, N_MOD * d)
        assert cur_col == col_major(i)

        w1, w3, w2 = bf(ffn_w1[i, 0]), bf(ffn_w3[i, 0]), bf(ffn_w2[i, 0])
        h_l = _ffn_call(h_l, mod, norm_g[i, 0], w1, w3, w2, kidx=0, tm_pref=FFN_TM, **lat)
        h_c = _ffn_call(h_c, mod, norm_g[i, 0], w1, w3, w2, kidx=0, tm_pref=FFN_TM, **cx)

        if i % 2 == 0:
            e = i // 2
            in_w, ow = bf(conv_in_w[e]), bf(conv_out_w[e])
            cp = (conv_a_w[e], conv_a_b[e], conv_ln_g[e], conv_ln_b[e], conv_b_w[e], ow)
            a, cv, bg = _conv_in_call(h_l, mod, norm_g[i, 1], in_w, tm_pref=512, **lat)
            h_l = _conv_mix_call(a, cv, bg, h_l, mod, *cp, seq_len=seq, **lat)
            if not last:
                a, cv, bg = _conv_in_call(h_c, mod, norm_g[i, 1], in_w, tm_pref=512, **cx)
                h_c = _conv_mix_call(a, cv, bg, h_c, mod, *cp, seq_len=ctx_len, **cx)
        else:
            o = i // 2
            w_main, w_dt, hp, cw_tbl = _ssm_weights(
                ssm_in_w[o], ssm_conv_w[o], ssm_conv_b[o], ssm_dt_bias[o], ssm_a_log[o],
                di=di, gn=gn, n_heads=n_heads)
            sargs = dict(di=di, gn=gn, tm_pref=1024)
            z_l, xs_l, bc_l, dt_l = _ssm_in_call(h_l, mod, norm_g[i, 1], w_main, w_dt, cw_tbl,
                                                 seq_len=seq, **sargs, **lat)
            z_c, xs_c, bc_c, dt_c = _ssm_in_call(h_c, mod, norm_g[i, 1], w_main, w_dt, cw_tbl,
                                                 seq_len=ctx_len, **sargs, **cx)
            ys_l, ys_c = [], []
            for dirn in range(2):
                dsk = jnp.repeat(ssm_d[o, dirn], HEAD_DIM).reshape(1, di)
                kw = dict(reverse=dirn == 1, dirn=dirn, batch=bsz, n_heads=n_heads, n_groups=N_GROUPS)
                h0 = jnp.zeros((bsz, N_GROUPS, D_STATE, di // N_GROUPS), _F32)
                y_c, h_fin = _ssd_call(xs_c, bc_c, dt_c, hp, dsk, h0, seq_len=ctx_len, **kw)
                y_l, _ = _ssd_call(xs_l, bc_l, dt_l, hp, dsk, h_fin, seq_len=seq, **kw)
                ys_l.append(y_l)
                ys_c.append(y_c)
            oargs = dict(gw=di // N_GROUPS, tm_pref=1024)
            ow = bf(ssm_out_w[o])
            h_l = _ssm_out_call(ys_l[0], ys_l[1], z_l, h_l, mod, ssm_norm_g[o], ow, **oargs, **lat)
            if not last:
                h_c = _ssm_out_call(ys_c[0], ys_c[1], z_c, h_c, mod, ssm_norm_g[o], ow, **oargs, **cx)

        w1, w3, w2 = bf(ffn_w1[i, 1]), bf(ffn_w3[i, 1]), bf(ffn_w2[i, 1])
        flip = cur_col != col_major(i + 1)
        perm_nb = (n_rows if cur_col else GRID_W) if flip else 0
        h_l = _ffn_call(h_l, mod, norm_g[i, 2], w1, w3, w2, kidx=2, tm_pref=FFN_TM,
                        final_g=final_g if last else None, perm_nb=perm_nb, **lat)
        cur_col = cur_col != flip
        if not last:
            h_c = _ffn_call(h_c, mod, norm_g[i, 2], w1, w3, w2, kidx=2, tm_pref=FFN_TM, **cx)
    return h_l.reshape(bsz, seq, d)
```

```python
import functools

import jax
import jax.numpy as jnp
from jax import lax
from jax.experimental import pallas as pl
from jax.experimental.pallas import tpu as pltpu

GRID_W = 64
EPS = 1e-6
N_MOD = 9
HEAD_DIM = 64
D_STATE = 128
N_GROUPS = 8
CHUNK = 128
SSM_CONV_WIDTH = 4

LANES = 128
SUBLANES = 8
BF16_ROWS = 16
COL_TILE = 512
VMEM_LIMIT_BYTES = 60 * 1024 * 1024
MOD_ROWS = 16
PERM_PAD = 8
FFN_COL_SPLIT = 2
FFN_TM = 512

_F32 = jnp.float32
_BF16 = jnp.bfloat16
_NEG = -1e30
_LOG2E = 1.4426950408889634


def _silu(v):
    h = 0.5 * v
    return h + h * jnp.tanh(h)


def _dot(a, b):
    return jnp.dot(a, b, preferred_element_type=_F32)


def _modulated(x, g, shift, scale):
    ms = jnp.mean(x * x, axis=-1, keepdims=True)
    return (x * lax.rsqrt(ms + EPS) * g) * (1.0 + scale) + shift


def _params(*sem):
    return pltpu.CompilerParams(dimension_semantics=sem, vmem_limit_bytes=VMEM_LIMIT_BYTES)


def _pick(pref, total):
    t = min(pref, total)
    while total % t:
        t //= 2
    return t


def _mod_kernel(c_ref, w_ref, b_ref, o_ref):
    s = _silu(c_ref[...]).astype(_BF16)
    o_ref[...] = _dot(s, w_ref[...].astype(_BF16)) + b_ref[...]


def _mod_call(c_all, mod_w, mod_b):
    depth, d, nd = mod_w.shape
    tn = _pick(1024, nd)
    return pl.pallas_call(
        _mod_kernel,
        grid=(depth, nd // tn),
        in_specs=[
            pl.BlockSpec((MOD_ROWS, d), lambda l, j: (0, 0)),
            pl.BlockSpec((None, d, tn), lambda l, j: (l, 0, j)),
            pl.BlockSpec((None, 1, tn), lambda l, j: (l, 0, j)),
        ],
        out_specs=pl.BlockSpec((None, MOD_ROWS, tn), lambda l, j: (l, 0, j)),
        out_shape=jax.ShapeDtypeStruct((depth, MOD_ROWS, nd), _F32),
        compiler_params=_params("parallel", "parallel"),
        name="mod",
    )(c_all, mod_w, mod_b.reshape(depth, 1, nd))


def _mod_spec(d, kidx, row0, tiles_per_row):
    return pl.BlockSpec((1, 1, 3 * d), lambda t, *_: (row0 + t // tiles_per_row, 0, kidx))


def _ffn_kernel(x_ref, mod_ref, g_ref, w1_ref, w3_ref, w2_ref, *rest, final_norm, perm_nb):
    rest = list(rest)
    fg_ref = rest.pop(0) if final_norm else None
    o_ref, hm_ref = rest[0], rest[1]
    j = pl.program_id(1)
    tm, d = x_ref.shape
    lane_tiles = [slice(c * LANES, (c + 1) * LANES) for c in range(d // LANES)]
    if perm_nb:
        xt_ref, acc_ref = rest[2], rest[3]
        na = tm // perm_nb
        pitch = perm_nb + PERM_PAD
        groups = [(slice(a * perm_nb, (a + 1) * perm_nb), slice(a * pitch, a * pitch + perm_nb))
                  for a in range(na)]

    @pl.when(j == 0)
    def _():
        m = mod_ref[0]
        x = x_ref[...]
        hm_ref[...] = _modulated(x, g_ref[...], m[:, :d], m[:, d:2 * d]).astype(_BF16)
        if perm_nb:
            for c, cs in enumerate(lane_tiles):
                for src, dst in groups:
                    xt_ref[c, dst, :] = x[src, cs]
            acc_ref[...] = jnp.zeros_like(acc_ref)
        else:
            o_ref[...] = jnp.zeros_like(o_ref)

    hm = hm_ref[...]
    tf = w1_ref.shape[1]
    part = None
    for q0 in range(0, tf, tf // FFN_COL_SPLIT):
        qs = slice(q0, q0 + tf // FFN_COL_SPLIT)
        a = _dot(hm, w1_ref[:, qs])
        b = _dot(hm, w3_ref[:, qs])
        pq = _dot((_silu(a) * b).astype(_BF16), w2_ref[qs, :])
        part = pq if part is None else part + pq
    if perm_nb:
        for c, cs in enumerate(lane_tiles):
            for src, dst in groups:
                acc_ref[c, dst, :] += part[src, cs]
    else:
        o_ref[...] += part

    @pl.when(j == pl.num_programs(1) - 1)
    def _():
        gate = 0.5 * mod_ref[0][:, 2 * d:]

        def finish(xv, av):
            h = xv + gate * av
            if final_norm:
                ms = jnp.mean(h * h, axis=-1, keepdims=True)
                h = h * lax.rsqrt(ms + EPS) * fg_ref[...]
            return h

        if perm_nb:
            for bi in range(perm_nb):
                rows = pl.ds(bi, na, stride=pitch)
                gather = lambda ref: jnp.concatenate(
                    [ref[c, rows, :] for c in range(len(lane_tiles))], axis=1)
                o_ref[bi] = finish(gather(xt_ref), gather(acc_ref))
        else:
            o_ref[...] = finish(x_ref[...], o_ref[...])


def _ffn_call(h, mod, g, w1, w3, w2, *, kidx, row0, rows_per_mod, tm_pref, final_g=None, perm_nb=0):
    t_rows, d = h.shape
    f = w1.shape[1]
    tm = _pick(tm_pref, rows_per_mod)
    tf = _pick(COL_TILE, f)
    tps = rows_per_mod // tm
    in_specs = [
        pl.BlockSpec((tm, d), lambda t, j: (t, 0)),
        _mod_spec(d, kidx, row0, tps),
        pl.BlockSpec((1, d), lambda t, j: (0, 0)),
        pl.BlockSpec((d, tf), lambda t, j: (0, j)),
        pl.BlockSpec((d, tf), lambda t, j: (0, j)),
        pl.BlockSpec((tf, d), lambda t, j: (j, 0)),
    ]
    args = [h, mod, g.reshape(1, d), w1, w3, w2]
    if final_g is not None:
        in_specs.append(pl.BlockSpec((1, d), lambda t, j: (0, 0)))
        args.append(final_g.reshape(1, d))
    scratch = [pltpu.VMEM((tm, d), _BF16)]
    if perm_nb:
        na = tm // perm_nb
        assert tm % perm_nb == 0 and na % SUBLANES == 0
        out_spec = pl.BlockSpec((None, perm_nb, na, d), lambda t, j: (t // tps, 0, t % tps, 0))
        out_shape = jax.ShapeDtypeStruct(
            (t_rows // rows_per_mod, perm_nb, rows_per_mod // perm_nb, d), _F32)
        scratch += [pltpu.VMEM((d // LANES, na * (perm_nb + PERM_PAD), LANES), _F32)] * 2
    else:
        out_spec = pl.BlockSpec((tm, d), lambda t, j: (t, 0))
        out_shape = jax.ShapeDtypeStruct((t_rows, d), _F32)
    out = pl.pallas_call(
        functools.partial(_ffn_kernel, final_norm=final_g is not None, perm_nb=perm_nb),
        grid=(t_rows // tm, f // tf),
        in_specs=in_specs,
        out_specs=out_spec,
        out_shape=out_shape,
        scratch_shapes=scratch,
        compiler_params=_params("parallel", "arbitrary"),
        name="ffn",
    )(*args)
    return out.reshape(t_rows, d)


def _conv_in_kernel(x_ref, mod_ref, g_ref, wa_ref, wag_ref, wbg_ref, wcg_ref, wv_ref,
                    a_ref, cv_ref, bg_ref, hm_ref):
    d = x_ref.shape[1]

    @pl.when(pl.program_id(1) == 0)
    def _():
        m = mod_ref[0]
        hm_ref[...] = _modulated(x_ref[...], g_ref[...], m[:, :d], m[:, d:2 * d]).astype(_BF16)

    hm = hm_ref[...]
    a_ref[...] = _dot(hm, wa_ref[...]) / (1.0 + jnp.exp(-_dot(hm, wag_ref[...])))
    cv_ref[...] = _dot(hm, wcg_ref[...]) * _dot(hm, wv_ref[...])
    bg_ref[...] = _dot(hm, wbg_ref[...])


def _conv_in_call(h, mod, g, in_w, *, row0, rows_per_mod, tm_pref):
    t_rows, d = h.shape
    ca = in_w.shape[1] // 5
    tm = _pick(tm_pref, rows_per_mod)
    tn = _pick(COL_TILE, ca)
    nj = ca // tn

    def wspec(p):
        return pl.BlockSpec((d, tn), lambda t, j: (0, p * nj + j))

    out_spec = pl.BlockSpec((tm, tn), lambda t, j: (t, j))
    out_shape = jax.ShapeDtypeStruct((t_rows, ca), _F32)
    return pl.pallas_call(
        _conv_in_kernel,
        grid=(t_rows // tm, nj),
        in_specs=[
            pl.BlockSpec((tm, d), lambda t, j: (t, 0)),
            _mod_spec(d, 1, row0, rows_per_mod // tm),
            pl.BlockSpec((1, d), lambda t, j: (0, 0)),
            wspec(0), wspec(1), wspec(2), wspec(3), wspec(4),
        ],
        out_specs=[out_spec, out_spec, out_spec],
        out_shape=[out_shape, out_shape, out_shape],
        scratch_shapes=[pltpu.VMEM((tm, d), _BF16)],
        compiler_params=_params("parallel", "arbitrary"),
        name="conv_in",
    )(h, mod, g.reshape(1, d), in_w, in_w, in_w, in_w, in_w)


A_HALO = 16
B_HALO = 8


def _conv_mix_kernel(a_ref, ap_ref, an_ref, cv_ref, cvp_ref, cvn_ref, bg_ref, x_ref, mod_ref,
                     wa_ref, ba_ref, lg_ref, lb_ref, wb_ref, ow_ref, o_ref,
                     aext_ref, cvext_ref, ac_ref, *, tiles_per_seq, ka, kb):
    tm, ca = a_ref.shape
    d = x_ref.shape[1]
    pos = pl.program_id(0) % tiles_per_seq
    has_prev = pos > 0
    has_next = pos < tiles_per_seq - 1

    aext_ref[0:A_HALO, :] = jnp.where(has_prev, ap_ref[...], 0.0)
    aext_ref[A_HALO:A_HALO + tm, :] = a_ref[...]
    aext_ref[A_HALO + tm:, :] = jnp.where(has_next, an_ref[...], 0.0)
    cvext_ref[0:B_HALO, :] = jnp.where(has_prev, cvp_ref[...], 0.0)
    cvext_ref[B_HALO:B_HALO + tm, :] = cv_ref[...]
    cvext_ref[B_HALO + tm:, :] = jnp.where(has_next, cvn_ref[...], 0.0)

    n_ext = tm + 2 * A_HALO
    for c in range(ca // LANES):
        cs = slice(c * LANES, (c + 1) * LANES)
        ext = aext_ref[:, cs]
        phases = [ext] + [pltpu.roll(ext, n_ext - r, 0) for r in range(1, SUBLANES)]
        acc = jnp.zeros((tm, LANES), _F32) + ba_ref[:, cs]
        for k in range(ka):
            off = A_HALO - ka // 2 + k
            base = off - off % SUBLANES
            acc = acc + phases[off % SUBLANES][base:base + tm, :] * wa_ref[k:k + 1, cs]
        ac_ref[:, cs] = acc

    a = ac_ref[...]
    mu = jnp.mean(a, axis=-1, keepdims=True)
    ctr = a - mu
    var = jnp.mean(ctr * ctr, axis=-1, keepdims=True)
    a = _silu(ctr * lax.rsqrt(var + EPS) * lg_ref[...] + lb_ref[...])

    cvc = jnp.zeros((tm, ca), _F32)
    for k in range(kb):
        cvc = cvc + cvext_ref[pl.ds(B_HALO - kb // 2 + k, tm), :] * wb_ref[k:k + 1, :]
    bq = bg_ref[...] * cvc

    mix = _dot(a.astype(_BF16), ow_ref[0:ca, :]) + _dot(bq.astype(_BF16), ow_ref[ca:, :])
    o_ref[...] = x_ref[...] + mod_ref[0][:, 2 * d:] * mix


def _conv_mix_call(a, cv, bg, h, mod, wa, ba, lg, lb, wb, ow, *, row0, rows_per_mod, seq_len):
    t_rows, ca = a.shape
    d = h.shape[1]
    ka, kb = wa.shape[0], wb.shape[0]
    tm = _pick(256, seq_len)
    nha, nhb = tm // A_HALO, tm // B_HALO
    last_a, last_b = t_rows // A_HALO - 1, t_rows // B_HALO - 1
    wa_p = jnp.zeros((32, ca), _F32).at[:ka].set(wa)
    wb_p = jnp.zeros((SUBLANES, ca), _F32).at[:kb].set(wb)
    row = lambda v: v.reshape(1, ca)
    full = lambda r, c: pl.BlockSpec((r, c), lambda t: (0, 0))
    return pl.pallas_call(
        functools.partial(_conv_mix_kernel, tiles_per_seq=seq_len // tm, ka=ka, kb=kb),
        grid=(t_rows // tm,),
        in_specs=[
            pl.BlockSpec((tm, ca), lambda t: (t, 0)),
            pl.BlockSpec((A_HALO, ca), lambda t: (jnp.maximum(t * nha - 1, 0), 0)),
            pl.BlockSpec((A_HALO, ca), lambda t: (jnp.minimum((t + 1) * nha, last_a), 0)),
            pl.BlockSpec((tm, ca), lambda t: (t, 0)),
            pl.BlockSpec((B_HALO, ca), lambda t: (jnp.maximum(t * nhb - 1, 0), 0)),
            pl.BlockSpec((B_HALO, ca), lambda t: (jnp.minimum((t + 1) * nhb, last_b), 0)),
            pl.BlockSpec((tm, ca), lambda t: (t, 0)),
            pl.BlockSpec((tm, d), lambda t: (t, 0)),
            _mod_spec(d, 1, row0, rows_per_mod // tm),
            full(32, ca), full(1, ca), full(1, ca), full(1, ca), full(SUBLANES, ca),
            full(2 * ca, d),
        ],
        out_specs=pl.BlockSpec((tm, d), lambda t: (t, 0)),
        out_shape=jax.ShapeDtypeStruct((t_rows, d), _F32),
        scratch_shapes=[
            pltpu.VMEM((tm + 2 * A_HALO, ca), _F32),
            pltpu.VMEM((tm + 2 * B_HALO, ca), _F32),
            pltpu.VMEM((tm, ca), _F32),
        ],
        compiler_params=_params("parallel"),
        name="conv_mix",
    )(a, a, a, cv, cv, cv, bg, h, mod, wa_p, row(ba), row(lg), row(lb), wb_p, ow)


S_HALO = BF16_ROWS
TAP_ROWS = 16
BWD_TAPS = 8
SSM_IN_ROW_BLOCK = 1024


def _ssm_in_kernel(x_ref, xp_ref, xn_ref, mod_ref, g_ref, w_ref, wdt_ref, cw_ref,
                   z_ref, xs_ref, bc_ref, dt_ref, hm_ref, *, nz, nx, nbd, tiles_per_seq):
    tm, d = x_ref.shape
    kw = SSM_CONV_WIDTH
    j = pl.program_id(1)
    pos = pl.program_id(0) % tiles_per_seq
    mid = slice(S_HALO, S_HALO + tm)

    @pl.when(j == 0)
    def _():
        m = mod_ref[0]
        mod = lambda v: _modulated(v, g_ref[...], m[:, :d], m[:, d:2 * d]).astype(_BF16)
        hm_ref[0:S_HALO, :] = mod(xp_ref[...])
        hm_ref[mid, :] = mod(x_ref[...])
        hm_ref[S_HALO + tm:, :] = mod(xn_ref[...])
        dt_ref[...] = _dot(hm_ref[mid, :], wdt_ref[...])

    rb = min(SSM_IN_ROW_BLOCK, tm)
    n_ext = rb + 2 * S_HALO

    def conv_act(dsts):
        for lo in range(0, tm, rb):
            u = _dot(hm_ref[lo:lo + n_ext, :], w_ref[...])
            if lo == 0:
                u = jnp.concatenate([jnp.where(pos > 0, u[0:S_HALO, :], 0.0), u[S_HALO:, :]], axis=0)
            if lo + rb == tm:
                u = jnp.concatenate([u[0:S_HALO + rb, :],
                                     jnp.where(pos < tiles_per_seq - 1, u[S_HALO + rb:, :], 0.0)], axis=0)
            for c in range(COL_TILE // LANES):
                cs = slice(c * LANES, (c + 1) * LANES)
                ext = u[:, cs]
                phases = {0: ext}
                for dst_ref, reverse in dsts:
                    base = BWD_TAPS if reverse else 0
                    acc = jnp.zeros((rb, LANES), _F32) + cw_ref[base + kw:base + kw + 1, cs]
                    for t in range(kw):
                        off = S_HALO + (kw - 1 - t) if reverse else S_HALO - (kw - 1) + t
                        r = off % SUBLANES
                        if r not in phases:
                            phases[r] = pltpu.roll(ext, n_ext - r, 0)
                        acc = acc + phases[r][off - r:off - r + rb, :] * cw_ref[base + t:base + t + 1, cs]
                    dst_ref[lo:lo + rb, cs] = _silu(acc).astype(_BF16)

    @pl.when(j < nz)
    def _():
        z_ref[...] = _dot(hm_ref[mid, :], w_ref[...]).astype(_BF16)

    @pl.when(jnp.logical_and(j >= nz, j < nz + nx))
    def _():
        conv_act([(xs_ref.at[0], False), (xs_ref.at[1], True)])

    @pl.when(jnp.logical_and(j >= nz + nx, j < nz + nx + nbd))
    def _():
        conv_act([(bc_ref, False)])

    @pl.when(j >= nz + nx + nbd)
    def _():
        conv_act([(bc_ref, True)])


def _ssm_in_call(h, mod, g, w_main, w_dt, cw_tbl, *, di, gn, row0, rows_per_mod, seq_len, tm_pref):
    t_rows, d = h.shape
    nz = nx = di // COL_TILE
    nbd = 2 * gn // COL_TILE
    tm = _pick(tm_pref, seq_len)
    nh = tm // S_HALO
    last_h = t_rows // S_HALO - 1
    z, xs, bc, dt = pl.pallas_call(
        functools.partial(_ssm_in_kernel, nz=nz, nx=nx, nbd=nbd, tiles_per_seq=seq_len // tm),
        grid=(t_rows // tm, nz + nx + 2 * nbd),
        in_specs=[
            pl.BlockSpec((tm, d), lambda t, j: (t, 0)),
            pl.BlockSpec((S_HALO, d), lambda t, j: (jnp.maximum(t * nh - 1, 0), 0)),
            pl.BlockSpec((S_HALO, d), lambda t, j: (jnp.minimum((t + 1) * nh, last_h), 0)),
            _mod_spec(d, 1, row0, rows_per_mod // tm),
            pl.BlockSpec((1, d), lambda t, j: (0, 0)),
            pl.BlockSpec((d, COL_TILE), lambda t, j: (0, j)),
            pl.BlockSpec((d, LANES), lambda t, j: (0, 0)),
            pl.BlockSpec((None, TAP_ROWS, COL_TILE), lambda t, j: (j, 0, 0)),
        ],
        out_specs=[
            pl.BlockSpec((tm, COL_TILE), lambda t, j: (t, jnp.minimum(j, nz - 1))),
            pl.BlockSpec((2, None, tm, COL_TILE), lambda t, j: (0, jnp.clip(j - nz, 0, nx - 1), t, 0)),
            pl.BlockSpec((None, tm, COL_TILE),
                         lambda t, j: (jnp.clip(j - nz - nx, 0, 2 * nbd - 1), t, 0)),
            pl.BlockSpec((tm, LANES), lambda t, j: (t, 0)),
        ],
        out_shape=[
            jax.ShapeDtypeStruct((t_rows, di), _BF16),
            jax.ShapeDtypeStruct((2, nx, t_rows, COL_TILE), _BF16),
            jax.ShapeDtypeStruct((2 * nbd, t_rows, COL_TILE), _BF16),
            jax.ShapeDtypeStruct((t_rows, LANES), _F32),
        ],
        scratch_shapes=[pltpu.VMEM((tm + 2 * S_HALO, d), _BF16)],
        compiler_params=_params("parallel", "arbitrary"),
        name="ssm_in",
    )(h, h, h, mod, g.reshape(1, d), w_main, w_dt, cw_tbl)
    return z, xs, bc, dt


def _time_scan(a, reverse):
    q = a.shape[0]
    row = lax.broadcasted_iota(jnp.int32, a.shape, 0)
    sh = 1
    while sh < q:
        if reverse:
            a = a + jnp.where(row < q - sh, pltpu.roll(a, q - sh, 0), 0.0)
        else:
            a = a + jnp.where(row >= sh, pltpu.roll(a, sh, 0), 0.0)
        sh *= 2
    return a


def _ssd_kernel(xs_ref, bc_ref, dt_ref, hp_ref, dsk_ref, h0_ref, y_ref, hfin_ref, hst_ref,
                *, reverse, dirn, n_heads, n_groups, nsub):
    q = CHUNK
    n = D_STATE
    gw = (n_heads // n_groups) * HEAD_DIM
    gn = n_groups * n
    k = pl.program_id(1)

    @pl.when(k == 0)
    def _():
        hst_ref[...] = h0_ref[...]

    rowi = lax.broadcasted_iota(jnp.int32, (q, q), 0)
    coli = lax.broadcasted_iota(jnp.int32, (q, q), 1)
    tri = (coli >= rowi) if reverse else (coli <= rowi)
    lo_half = lax.broadcasted_iota(jnp.int32, (q, LANES), 1) < HEAD_DIM
    lo_row = lax.broadcasted_iota(jnp.int32, (1, LANES), 1) < HEAD_DIM
    sel_lo = jnp.where(lo_half, 1.0, 0.0).astype(_BF16)
    sel_hi = jnp.where(lo_half, 0.0, 1.0).astype(_BF16)
    edge = 0 if reverse else q - 1

    def chunk_body(ii, carry):
        i = (nsub - 1 - ii) if reverse else ii
        r0 = pl.multiple_of(i * q, q)
        rows = pl.ds(r0, q)

        dtv = dt_ref[rows, :] + hp_ref[0:1, :]
        dtv = jnp.maximum(dtv, 0.0) + jnp.log(1.0 + jnp.exp(-jnp.abs(dtv)))
        c2 = _time_scan(dtv * (-_LOG2E * jnp.exp(hp_ref[1:2, :])), reverse)
        c2_t = c2.T
        lq_t = c2_t - jnp.log2(dtv.T)
        e_tot = jnp.exp2(c2[edge:edge + 1, :])
        w_t = jnp.exp2(c2_t[:, edge:edge + 1] - lq_t)

        for g in range(n_groups):
            x0 = g * gw
            xg = xs_ref[x0 // COL_TILE, rows, x0 % COL_TILE:x0 % COL_TILE + gw]
            b_off, c_off = g * n, gn + g * n
            bg = bc_ref[b_off // COL_TILE, rows, b_off % COL_TILE:b_off % COL_TILE + n]
            cg = bc_ref[c_off // COL_TILE, rows, c_off % COL_TILE:c_off % COL_TILE + n]
            s = lax.dot_general(cg, bg, (((1,), (1,)), ((), ())),
                                preferred_element_type=_F32)
            inter = _dot(cg, hst_ref[g].astype(_BF16))
            bg_t = bg.astype(_F32).T.astype(_BF16)

            for p in range(gw // LANES):
                ls = slice(p * LANES, (p + 1) * LANES)
                xp = xg[:, ls]
                ln0 = dirn * n_heads + x0 // HEAD_DIM + 2 * p
                lhs_m, lhs_b, cols = [], [], []
                for ln in (ln0, ln0 + 1):
                    col = jnp.broadcast_to(c2[:, ln:ln + 1], (q, q))
                    lm = jnp.exp2(jnp.where(tri, col - lq_t[ln:ln + 1, :], _NEG))
                    lhs_m.append((s * lm).astype(_BF16))
                    lhs_b.append(bg_t * jnp.broadcast_to(w_t[ln:ln + 1, :], (n, q)).astype(_BF16))
                    cols.append(col)
                lhs = jnp.concatenate([jnp.concatenate(lhs_m, axis=1),
                                       jnp.concatenate(lhs_b, axis=1)], axis=0)
                rhs = jnp.concatenate([xp * sel_lo, xp * sel_hi], axis=0)
                res = _dot(lhs, rhs)
                decay = jnp.exp2(jnp.where(lo_half, cols[0], cols[1]))
                y = (res[0:q, :] + decay * inter[:, ls]
                     + dsk_ref[:, x0 + p * LANES:x0 + (p + 1) * LANES] * xp.astype(_F32))
                yl = (x0 + p * LANES) % COL_TILE
                y_ref[(x0 + p * LANES) // COL_TILE, rows, yl:yl + LANES] = y.astype(_BF16)
                e2 = jnp.where(lo_row, jnp.broadcast_to(e_tot[:, ln0:ln0 + 1], (1, LANES)),
                               jnp.broadcast_to(e_tot[:, ln0 + 1:ln0 + 2], (1, LANES)))
                hst_ref[g, :, ls] = hst_ref[g, :, ls] * e2 + res[q:, :]
        return carry

    lax.fori_loop(0, nsub, chunk_body, 0)

    @pl.when(k == pl.num_programs(1) - 1)
    def _():
        hfin_ref[...] = hst_ref[...]


def _ssd_call(xs, bc, dt, hp, dsk, h0, *, reverse, dirn, batch, seq_len, n_heads, n_groups):
    _, nx, t_rows, _ = xs.shape
    nbd = bc.shape[0] // 2
    di = n_heads * HEAD_DIM
    gw = di // n_groups
    rows = _pick(4 * CHUNK, seq_len)
    nblk = seq_len // rows

    def blk(b, k):
        return b * nblk + ((nblk - 1 - k) if reverse else k)

    state_spec = pl.BlockSpec((None, n_groups, D_STATE, gw), lambda b, k: (b, 0, 0, 0))
    y, hfin = pl.pallas_call(
        functools.partial(_ssd_kernel, reverse=reverse, dirn=dirn, n_heads=n_heads,
                          n_groups=n_groups, nsub=rows // CHUNK),
        grid=(batch, nblk),
        in_specs=[
            pl.BlockSpec((None, nx, rows, COL_TILE), lambda b, k: (dirn, 0, blk(b, k), 0)),
            pl.BlockSpec((nbd, rows, COL_TILE), lambda b, k: (dirn, blk(b, k), 0)),
            pl.BlockSpec((rows, LANES), lambda b, k: (blk(b, k), 0)),
            pl.BlockSpec((SUBLANES, LANES), lambda b, k: (0, 0)),
            pl.BlockSpec((1, di), lambda b, k: (0, 0)),
            state_spec,
        ],
        out_specs=[
            pl.BlockSpec((nx, rows, COL_TILE), lambda b, k: (0, blk(b, k), 0)),
            state_spec,
        ],
        out_shape=[
            jax.ShapeDtypeStruct((nx, t_rows, COL_TILE), _BF16),
            jax.ShapeDtypeStruct((batch, n_groups, D_STATE, gw), _F32),
        ],
        scratch_shapes=[pltpu.VMEM((n_groups, D_STATE, gw), _F32)],
        compiler_params=_params("parallel", "arbitrary"),
        name="ssd",
    )(xs, bc, dt, hp, dsk, h0)
    return y, hfin


def _ssm_out_kernel(yf_ref, yb_ref, z_ref, x_ref, mod_ref, ng_ref, ow_ref, o_ref, *, gw):
    d = x_ref.shape[1]
    j = pl.program_id(1)

    @pl.when(j == 0)
    def _():
        o_ref[...] = jnp.zeros_like(o_ref)

    y = yf_ref[...].astype(_F32) + yb_ref[...].astype(_F32)
    y = y * _silu(z_ref[...].astype(_F32))
    parts = []
    for s in range(COL_TILE // gw):
        ys = y[:, s * gw:(s + 1) * gw]
        parts.append(ys * lax.rsqrt(jnp.mean(ys * ys, axis=-1, keepdims=True) + EPS))
    y = (parts[0] if len(parts) == 1 else jnp.concatenate(parts, axis=1)) * ng_ref[...]
    o_ref[...] += _dot(y.astype(_BF16), ow_ref[...])

    @pl.when(j == pl.num_programs(1) - 1)
    def _():
        o_ref[...] = x_ref[...] + mod_ref[0][:, 2 * d:] * o_ref[...]


def _ssm_out_call(yf, yb, z, h, mod, ng, ow, *, gw, row0, rows_per_mod, tm_pref):
    t_rows, d = h.shape
    nx = yf.shape[0]
    di = nx * COL_TILE
    tm = _pick(tm_pref, rows_per_mod)
    y_spec = pl.BlockSpec((None, tm, COL_TILE), lambda t, j: (j, t, 0))
    return pl.pallas_call(
        functools.partial(_ssm_out_kernel, gw=min(gw, COL_TILE)),
        grid=(t_rows // tm, nx),
        in_specs=[
            y_spec, y_spec,
            pl.BlockSpec((tm, COL_TILE), lambda t, j: (t, j)),
            pl.BlockSpec((tm, d), lambda t, j: (t, 0)),
            _mod_spec(d, 1, row0, rows_per_mod // tm),
            pl.BlockSpec((1, COL_TILE), lambda t, j: (0, j)),
            pl.BlockSpec((COL_TILE, d), lambda t, j: (j, 0)),
        ],
        out_specs=pl.BlockSpec((tm, d), lambda t, j: (t, 0)),
        out_shape=jax.ShapeDtypeStruct((t_rows, d), _F32),
        compiler_params=_params("parallel", "arbitrary"),
        name="ssm_out",
    )(yf, yb, z, h, mod, ng.reshape(1, di), ow)


def _ssm_weights(w_in, conv_w, conv_b, dt_bias, a_log, *, di, gn, n_heads):
    d = w_in.shape[0]
    dcols = 2 * gn + n_heads
    f0 = 2 * di
    b0 = f0 + dcols
    w_main = jnp.concatenate(
        [w_in[:, :f0], w_in[:, f0:f0 + 2 * gn], w_in[:, b0:b0 + 2 * gn]], axis=1).astype(_BF16)
    w_dt = jnp.zeros((d, LANES), _F32)
    w_dt = w_dt.at[:, :n_heads].set(w_in[:, f0 + 2 * gn:f0 + dcols])
    w_dt = w_dt.at[:, n_heads:2 * n_heads].set(w_in[:, b0 + 2 * gn:b0 + dcols]).astype(_BF16)
    hp = jnp.zeros((SUBLANES, LANES), _F32)
    hp = hp.at[0, :2 * n_heads].set(dt_bias.reshape(-1))
    hp = hp.at[1, :2 * n_heads].set(a_log.reshape(-1))

    kw = SSM_CONV_WIDTH
    half = jnp.zeros((BWD_TAPS, COL_TILE), _F32)

    def taps(dirn, lo):
        t = half.at[:kw].set(conv_w[dirn][:, lo:lo + COL_TILE])
        return t.at[kw].set(conv_b[dirn][lo:lo + COL_TILE])

    nz = nx = di // COL_TILE
    nbd = 2 * gn // COL_TILE
    blocks = [jnp.concatenate([half, half])] * nz
    blocks += [jnp.concatenate([taps(0, c * COL_TILE), taps(1, c * COL_TILE)]) for c in range(nx)]
    blocks += [jnp.concatenate([taps(0, di + c * COL_TILE), half]) for c in range(nbd)]
    blocks += [jnp.concatenate([half, taps(1, di + c * COL_TILE)]) for c in range(nbd)]
    return w_main, w_dt, hp, jnp.stack(blocks)


def kernel(x, c, ctx, c_ctx, mod_w, mod_b, norm_g, ffn_w1, ffn_w3, ffn_w2, conv_in_w, conv_a_w, conv_a_b, conv_ln_g, conv_ln_b, conv_b_w, conv_out_w, ssm_in_w, ssm_conv_w, ssm_conv_b, ssm_dt_bias, ssm_a_log, ssm_d, ssm_norm_g, ssm_out_w, final_g):
    bsz, seq, d = x.shape
    ctx_len = ctx.shape[1]
    depth = mod_w.shape[0]
    di = ssm_out_w.shape[1]
    n_heads = di // HEAD_DIM
    gn = N_GROUPS * D_STATE
    n_rows = seq // GRID_W
    assert bsz + 1 <= MOD_ROWS and 2 * n_heads <= LANES
    assert seq % GRID_W == 0 and di % COL_TILE == 0 and (2 * gn) % COL_TILE == 0

    h_l = x.reshape(bsz * seq, d)
    h_c = ctx.reshape(bsz * ctx_len, d)
    c_all = jnp.zeros((MOD_ROWS, d), _F32).at[:bsz].set(c).at[bsz].set(c_ctx)
    mod_all = _mod_call(c_all, mod_w, mod_b)

    lat = dict(row0=0, rows_per_mod=seq)
    cx = dict(row0=bsz, rows_per_mod=bsz * ctx_len)
    bf = lambda w: w.astype(_BF16)

    def col_major(i):
        return i < depth and i % 2 == 1 and (i // 2) % 2 == 1

    assert not col_major(0)
    cur_col = False
    for i in range(depth):
        last = i == depth - 1
        mod = mod_all[i].reshape(MOD_ROWS, 1, N_MOD * d)
        assert cur_col == col_major(i)

        w1, w3, w2 = bf(ffn_w1[i, 0]), bf(ffn_w3[i, 0]), bf(ffn_w2[i, 0])
        h_l = _ffn_call(h_l, mod, norm_g[i, 0], w1, w3, w2, kidx=0, tm_pref=FFN_TM, **lat)
        h_c = _ffn_call(h_c, mod, norm_g[i, 0], w1, w3, w2, kidx=0, tm_pref=FFN_TM, **cx)

        if i % 2 == 0:
            e = i // 2
            in_w, ow = bf(conv_in_w[e]), bf(conv_out_w[e])
            cp = (conv_a_w[e], conv_a_b[e], conv_ln_g[e], conv_ln_b[e], conv_b_w[e], ow)
            a, cv, bg = _conv_in_call(h_l, mod, norm_g[i, 1], in_w, tm_pref=512, **lat)
            h_l = _conv_mix_call(a, cv, bg, h_l, mod, *cp, seq_len=seq, **lat)
            if not last:
                a, cv, bg = _conv_in_call(h_c, mod, norm_g[i, 1], in_w, tm_pref=512, **cx)
                h_c = _conv_mix_call(a, cv, bg, h_c, mod, *cp, seq_len=ctx_len, **cx)
        else:
            o = i // 2
            w_main, w_dt, hp, cw_tbl = _ssm_weights(
                ssm_in_w[o], ssm_conv_w[o], ssm_conv_b[o], ssm_dt_bias[o], ssm_a_log[o],
                di=di, gn=gn, n_heads=n_heads)
            sargs = dict(di=di, gn=gn, tm_pref=1024)
            z_l, xs_l, bc_l, dt_l = _ssm_in_call(h_l, mod, norm_g[i, 1], w_main, w_dt, cw_tbl,
                                                 seq_len=seq, **sargs, **lat)
            z_c, xs_c, bc_c, dt_c = _ssm_in_call(h_c, mod, norm_g[i, 1], w_main, w_dt, cw_tbl,
                                                 seq_len=ctx_len, **sargs, **cx)
            ys_l, ys_c = [], []
            for dirn in range(2):
                dsk = jnp.repeat(ssm_d[o, dirn], HEAD_DIM).reshape(1, di)
                kw = dict(reverse=dirn == 1, dirn=dirn, batch=bsz, n_heads=n_heads, n_groups=N_GROUPS)
                h0 = jnp.zeros((bsz, N_GROUPS, D_STATE, di // N_GROUPS), _F32)
                y_c, h_fin = _ssd_call(xs_c, bc_c, dt_c, hp, dsk, h0, seq_len=ctx_len, **kw)
                y_l, _ = _ssd_call(xs_l, bc_l, dt_l, hp, dsk, h_fin, seq_len=seq, **kw)
                ys_l.append(y_l)
                ys_c.append(y_c)
            oargs = dict(gw=di // N_GROUPS, tm_pref=1024)
            ow = bf(ssm_out_w[o])
            h_l = _ssm_out_call(ys_l[0], ys_l[1], z_l, h_l, mod, ssm_norm_g[o], ow, **oargs, **lat)
            if not last:
                h_c = _ssm_out_call(ys_c[0], ys_c[1], z_c, h_c, mod, ssm_norm_g[o], ow, **oargs, **cx)

        w1, w3, w2 = bf(ffn_w1[i, 1]), bf(ffn_w3[i, 1]), bf(ffn_w2[i, 1])
        flip = cur_col != col_major(i + 1)
        perm_nb = (n_rows if cur_col else GRID_W) if flip else 0
        h_l = _ffn_call(h_l, mod, norm_g[i, 2], w1, w3, w2, kidx=2, tm_pref=FFN_TM,
                        final_g=final_g if last else None, perm_nb=perm_nb, **lat)
        cur_col = cur_col != flip
        if not last:
            h_c = _ffn_call(h_c, mod, norm_g[i, 2], w1, w3, w2, kidx=2, tm_pref=FFN_TM, **cx)
    return h_l.reshape(bsz, seq, d)
```
